```python
import math
import jax, jax.numpy as jnp
from jax import lax
import numpy as np

D_MODEL = 1024
BATCH = 4
SEQ = 4096
DEPTH = 4

N_MIXERS = 4
N_S5 = (DEPTH + 3) // 4
N_HGRN = (DEPTH + 2) // 4
N_GDN = (DEPTH + 1) // 4
N_MOBA = DEPTH // 4
D_FF = 2816
RMS_EPS = 1e-6
S5_GROUP = 16
S5_GROUPS = D_MODEL // S5_GROUP
S5_STATE = 64
HG_HEADS = 8
HG_DK = D_MODEL // HG_HEADS
HG_DV = D_MODEL // HG_HEADS
HG_CHUNK = 32
HG_IN = 2 * HG_HEADS * HG_DK + 2 * HG_HEADS * HG_DV
GDN_QK_HEADS = 8
GDN_V_HEADS = 16
GDN_DK = 128
GDN_DV = 128
GDN_CONV = 4
GDN_CHUNK = 64
GDN_QKV = 2 * GDN_QK_HEADS * GDN_DK + GDN_V_HEADS * GDN_DV
GDN_IN = GDN_QKV + GDN_V_HEADS * GDN_DV + 2 * GDN_V_HEADS
MOBA_HEADS = 8
MOBA_DH = 128
MOBA_BLOCK = 256
MOBA_TOPK = 3
MOBA_QCHUNK = 32
REL_BUCKETS = 32
REL_MAX_DIST = 2048

kernel_name = 'hybrid_interleaved_s5_hgrn2_gdn_moba'


def _rms(x, g):
    xf = x.astype(jnp.float32)
    y = xf * lax.rsqrt(jnp.mean(xf * xf, axis=-1, keepdims=True) + RMS_EPS)
    return (y * g.astype(jnp.float32)).astype(x.dtype)


def _l2norm(t):
    return t * lax.rsqrt(jnp.sum(t * t, axis=-1, keepdims=True) + 1e-6)


def _swiglu(x, w_gate, w_up, w_down):
    return (jax.nn.silu(x @ w_gate) * (x @ w_up)) @ w_down


def _causal_conv(x, w):
    k_len, ch = w.shape
    return lax.conv_general_dilated(x, w[:, None, :].astype(x.dtype), window_strides=(1,),
                                    padding=[(k_len - 1, 0)],
                                    dimension_numbers=('NWC', 'WIO', 'NWC'),
                                    feature_group_count=ch)


def _t5_bucket(rel):
    n = jnp.maximum(rel, 0)
    max_exact = REL_BUCKETS // 2
    nf = jnp.maximum(n, 1).astype(jnp.float32)
    large = max_exact + (jnp.log(nf / max_exact) / math.log(REL_MAX_DIST / max_exact)
                         * (REL_BUCKETS - max_exact)).astype(jnp.int32)
    large = jnp.minimum(large, REL_BUCKETS - 1)
    return jnp.where(n < max_exact, n, large)


def _s5_mixer(u, lam_re, lam_im, log_dt, b_re, b_im, c_re, c_im, d_skip, w_glu):
    f32 = jnp.float32
    bsz, L, _ = u.shape
    lam = lax.complex(jnp.minimum(lam_re.astype(f32), -1e-4), lam_im.astype(f32))
    dt = jnp.exp(log_dt.astype(f32))[:, None]
    lam_bar = jnp.exp(lam * dt)
    b_bar = ((lam_bar - 1.0) / lam)[..., None] * lax.complex(b_re.astype(f32), b_im.astype(f32))
    ug = u.astype(f32).reshape(bsz, L, S5_GROUPS, S5_GROUP)
    bu = jnp.einsum('blgh,gph->lbgp', ug, b_bar)
    a = jnp.broadcast_to(lam_bar, (L, 1) + lam_bar.shape)

    def combine(e1, e2):
        a1, b1 = e1
        a2, b2 = e2
        return a1 * a2, a2 * b1 + b2

    _, h = lax.associative_scan(combine, (a, bu), axis=0)
    c = lax.complex(c_re.astype(f32), c_im.astype(f32))
    y = jnp.einsum('lbgp,ghp->blgh', h, c).real.reshape(bsz, L, D_MODEL)
    y = y + d_skip.astype(f32) * u.astype(f32)
    y = jax.nn.gelu(y).astype(u.dtype)
    val, gate = jnp.split(y @ w_glu, 2, axis=-1)
    return val * jax.nn.sigmoid(gate)


def _hgrn2_mixer(x, w_in, lb, norm_g, w_out):
    f32 = jnp.float32
    bsz, L, _ = x.shape
    H, dk, dv, C = HG_HEADS, HG_DK, HG_DV, HG_CHUNK
    N = L // C
    q, f, i, g = jnp.split(x @ w_in, [H * dk, 2 * H * dk, 2 * H * dk + H * dv], axis=-1)
    lb = lb.astype(f32)
    f = f.astype(f32)
    log_f = jnp.logaddexp(jnp.log(lb), jnp.log1p(-lb) + jax.nn.log_sigmoid(f))
    k = (1.0 - lb) * jax.nn.sigmoid(-f)
    q = jax.nn.silu(q.astype(f32)) * dk ** -0.5

    def chunk(t, d):
        return t.reshape(bsz, N, C, H, d).transpose(1, 0, 3, 2, 4)

    q, k, log_f = chunk(q, dk), chunk(k, dk), chunk(log_f, dk)
    v = chunk(i.astype(f32), dv)
    b = jnp.cumsum(log_f, axis=3)
    b_last = b[:, :, :, -1:, :]
    q_t = q * jnp.exp(b)
    k_t = k * jnp.exp(-b)
    k_state = k * jnp.exp(b_last - b)
    causal = jnp.tril(jnp.ones((C, C), bool))
    attn = jnp.where(causal, jnp.einsum('nbhtd,nbhsd->nbhts', q_t, k_t), 0.0)
    o_intra = jnp.einsum('nbhts,nbhsv->nbhtv', attn, v)

    def step(S, inp):
        qc, kc, vc, dc = inp
        o = jnp.einsum('bhtd,bhdv->bhtv', qc, S)
        S = S * dc[..., None] + jnp.einsum('bhsd,bhsv->bhdv', kc, vc)
        return S, o

    S0 = jnp.zeros((bsz, H, dk, dv), f32)
    _, o_inter = lax.scan(step, S0, (q_t, k_state, v, jnp.exp(b_last[:, :, :, 0, :])))
    o = (o_intra + o_inter).transpose(1, 0, 3, 2, 4).reshape(bsz, L, H, dv)
    o = _rms(o, norm_g).reshape(bsz, L, H * dv) * jax.nn.silu(g.astype(f32))
    return o.astype(x.dtype) @ w_out


def _gdn_mixer(x, w_in, conv_w, a_log, dt_bias, norm_g, w_out):
    f32 = jnp.float32
    bsz, L, _ = x.shape
    Hk, Hv, dk, dv, C = GDN_QK_HEADS, GDN_V_HEADS, GDN_DK, GDN_DV, GDN_CHUNK
    N = L // C
    qkv, z, beta_raw, a_raw = jnp.split(
        x @ w_in, [GDN_QKV, GDN_QKV + Hv * dv, GDN_QKV + Hv * dv + Hv], axis=-1)
    qkv = jax.nn.silu(_causal_conv(qkv, conv_w).astype(f32))
    q, k, v = jnp.split(qkv, [Hk * dk, 2 * Hk * dk], axis=-1)
    q = jnp.repeat(_l2norm(q.reshape(bsz, L, Hk, dk)), Hv // Hk, axis=2) * dk ** -0.5
    k = jnp.repeat(_l2norm(k.reshape(bsz, L, Hk, dk)), Hv // Hk, axis=2)
    v = v.reshape(bsz, L, Hv, dv)
    beta = jax.nn.sigmoid(beta_raw.astype(f32))
    g = -jnp.exp(a_log.astype(f32)) * jax.nn.softplus(a_raw.astype(f32) + dt_bias.astype(f32))

    def chunk(t):
        return t.reshape(bsz, N, C, Hv, -1).transpose(1, 0, 3, 2, 4)

    q, k, v = chunk(q), chunk(k), chunk(v)
    beta = chunk(beta[..., None])[..., 0]
    b = jnp.cumsum(chunk(g[..., None])[..., 0], axis=-1)
    incl = jnp.tril(jnp.ones((C, C), bool))
    strict = jnp.tril(jnp.ones((C, C), bool), -1)
    decay = jnp.exp(jnp.where(incl, b[..., :, None] - b[..., None, :], -jnp.inf))
    kb = k * beta[..., None]
    m = jnp.where(strict, jnp.einsum('nbhtd,nbhsd->nbhts', kb, k) * decay, 0.0)
    a_mat = m + jnp.eye(C, dtype=f32)
    rhs = jnp.concatenate([v * beta[..., None], kb * jnp.exp(b)[..., None]], axis=-1)
    sol = lax.linalg.triangular_solve(a_mat, rhs, left_side=True, lower=True, unit_diagonal=True)
    u_val, w_cum = sol[..., :dv], sol[..., dv:]
    qk = jnp.einsum('nbhtd,nbhsd->nbhts', q, k) * decay
    q_dec = q * jnp.exp(b)[..., None]
    b_last = b[..., -1:]
    k_state = k * jnp.exp(b_last - b)[..., None]

    def step(S, inp):
        qd, qkc, uc, wc, ksc, dl = inp
        v_new = uc - jnp.einsum('bhtd,bhdv->bhtv', wc, S)
        o = jnp.einsum('bhtd,bhdv->bhtv', qd, S) + jnp.einsum('bhts,bhsv->bhtv', qkc, v_new)
        S = S * dl[..., None] + jnp.einsum('bhsd,bhsv->bhdv', ksc, v_new)
        return S, o

    S0 = jnp.zeros((bsz, Hv, dk, dv), f32)
    _, o = lax.scan(step, S0, (q_dec, qk, u_val, w_cum, k_state, jnp.exp(b_last)))
    o = o.transpose(1, 0, 3, 2, 4).reshape(bsz, L, Hv, dv)
    o = _rms(o, norm_g) * jax.nn.silu(z.astype(f32)).reshape(bsz, L, Hv, dv)
    return o.reshape(bsz, L, Hv * dv).astype(x.dtype) @ w_out


def _moba_mixer(x, w_in, rel_table, w_out):
    f32 = jnp.float32
    bsz, L, _ = x.shape
    H, dh, BS, QC = MOBA_HEADS, MOBA_DH, MOBA_BLOCK, MOBA_QCHUNK
    q, k, v = jnp.split((x @ w_in).astype(f32), 3, axis=-1)

    def heads(t):
        return t.reshape(bsz, L, H, dh).transpose(0, 2, 1, 3)

    q, k, v = heads(q), heads(k), heads(v)
    NB = -(-L // BS)
    pad = NB * BS - L
    k = jnp.pad(k, ((0, 0), (0, 0), (0, pad), (0, 0)))
    v = jnp.pad(v, ((0, 0), (0, 0), (0, pad), (0, 0)))
    kb = k.reshape(bsz, H, NB, BS, dh)
    vb = v.reshape(bsz, H, NB, BS, dh)
    k_mean = kb.mean(axis=3)
    q_blk = jnp.arange(L) // BS
    gate = jnp.einsum('bhld,bhnd->bhln', q, k_mean)
    past = jnp.arange(NB)[None, :] < q_blk[:, None]
    gate = jnp.where(past, gate, -1e30)
    topk = min(MOBA_TOPK, NB)
    _, top_idx = lax.top_k(gate, topk)
    top_valid = top_idx < q_blk[:, None]
    scale = dh ** -0.5
    table_t = rel_table.astype(f32).T
    h_ix = jnp.arange(H)[None, :, None, None]
    b_ix3 = jnp.arange(bsz)[:, None, None]
    h_ix3 = jnp.arange(H)[None, :, None]

    def chunk_fn(c):
        s = c * QC
        qc = lax.dynamic_slice_in_dim(q, s, QC, axis=2)
        idx = lax.dynamic_slice_in_dim(top_idx, s, QC, axis=2)
        valid = lax.dynamic_slice_in_dim(top_valid, s, QC, axis=2)
        qpos = s + jnp.arange(QC)
        flat = idx.reshape(bsz, H, QC * topk)
        k_sel = kb[b_ix3, h_ix3, flat].reshape(bsz, H, QC, topk * BS, dh)
        v_sel = vb[b_ix3, h_ix3, flat].reshape(bsz, H, QC, topk * BS, dh)
        kpos_sel = (idx[..., None] * BS + jnp.arange(BS)).reshape(bsz, H, QC, topk * BS)
        mask_sel = jnp.repeat(valid, BS, axis=-1)
        ob = (s // BS) * BS
        k_own = lax.dynamic_slice_in_dim(k, ob, BS, axis=2)
        v_own = lax.dynamic_slice_in_dim(v, ob, BS, axis=2)
        kpos_own = ob + jnp.arange(BS)
        mask_own = kpos_own[None, :] <= qpos[:, None]
        logit_sel = (jnp.einsum('bhqd,bhqkd->bhqk', qc, k_sel) * scale
                     + table_t[h_ix, _t5_bucket(qpos[:, None] - kpos_sel)])
        logit_own = (jnp.einsum('bhqd,bhkd->bhqk', qc, k_own) * scale
                     + table_t[:, _t5_bucket(qpos[:, None] - kpos_own[None, :])][None])
        logits = jnp.concatenate([jnp.where(mask_sel, logit_sel, -jnp.inf),
                                  jnp.where(mask_own, logit_own, -jnp.inf)], axis=-1)
        p = jax.nn.softmax(logits, axis=-1)
        return (jnp.einsum('bhqk,bhqkd->bhqd', p[..., :topk * BS], v_sel)
                + jnp.einsum('bhqk,bhkd->bhqd', p[..., topk * BS:], v_own))

    outs = lax.map(chunk_fn, jnp.arange(L // QC))
    o = outs.transpose(1, 0, 3, 2, 4).reshape(bsz, L, H * dh)
    return o.astype(x.dtype) @ w_out


def setup_inputs(seed: int = 0) -> dict:
    key = jax.random.key(seed)
    ks = jax.random.split(key, 32)
    f32 = jnp.float32

    def nrm(i, shape, sc):
        return sc * jax.random.normal(ks[i], shape, f32)

    def gain(i, shape):
        return 1.0 + 0.02 * jax.random.normal(ks[i], shape, f32)

    D, G, P, Gs = D_MODEL, S5_GROUPS, S5_STATE, S5_GROUP
    Hv = GDN_V_HEADS
    x = nrm(0, (BATCH, SEQ, D), 1.0)
    norm_g = gain(1, (DEPTH, 3, D))
    final_norm_g = gain(2, (D,))
    ffn_w_gate = nrm(3, (DEPTH, 2, D, D_FF), D ** -0.5)
    ffn_w_up = nrm(4, (DEPTH, 2, D, D_FF), D ** -0.5)
    ffn_w_down = nrm(5, (DEPTH, 2, D_FF, D), D_FF ** -0.5)
    s5_lam_re = -0.5 + nrm(6, (N_S5, G, P), 0.01)
    s5_lam_im = math.pi * jnp.arange(P, dtype=f32) + nrm(7, (N_S5, G, P), 0.01)
    s5_log_dt = jax.random.uniform(ks[8], (N_S5, G), f32, math.log(1e-3), math.log(1e-1))
    s5_b_re = nrm(9, (N_S5, G, P, Gs), (2 * Gs) ** -0.5)
    s5_b_im = nrm(10, (N_S5, G, P, Gs), (2 * Gs) ** -0.5)
    s5_c_re = nrm(11, (N_S5, G, Gs, P), P ** -0.5)
    s5_c_im = nrm(12, (N_S5, G, Gs, P), P ** -0.5)
    s5_d = nrm(13, (N_S5, D), 1.0)
    s5_w_glu = nrm(14, (N_S5, D, 2 * D), D ** -0.5)
    hg_w_in = nrm(15, (N_HGRN, D, HG_IN), D ** -0.5)
    hg_lower_bounds = nrm(16, (DEPTH, HG_HEADS * HG_DK), 0.1)
    hg_norm_g = gain(17, (N_HGRN, HG_DV))
    hg_w_out = nrm(18, (N_HGRN, HG_HEADS * HG_DV, D), (HG_HEADS * HG_DV) ** -0.5)
    gdn_w_in = nrm(19, (N_GDN, D, GDN_IN), D ** -0.5)
    gdn_conv_w = nrm(20, (N_GDN, GDN_CONV, GDN_QKV), GDN_CONV ** -0.5)
    gdn_a_log = jnp.log(jax.random.uniform(ks[21], (N_GDN, Hv), f32, 1.0, 16.0))
    dt = jnp.exp(jax.random.uniform(ks[22], (N_GDN, Hv), f32, math.log(1e-3), math.log(1e-1)))
    gdn_dt_bias = dt + jnp.log(-jnp.expm1(-dt))
    gdn_norm_g = gain(23, (N_GDN, GDN_DV))
    gdn_w_out = nrm(24, (N_GDN, Hv * GDN_DV, D), (Hv * GDN_DV) ** -0.5)
    moba_w_in = nrm(25, (N_MOBA, D, 3 * MOBA_HEADS * MOBA_DH), D ** -0.5)
    moba_w_out = nrm(26, (N_MOBA, MOBA_HEADS * MOBA_DH, D), (MOBA_HEADS * MOBA_DH) ** -0.5)
    rel_bias_table = nrm(27, (REL_BUCKETS, MOBA_HEADS), 0.5)
    return {'x': x, 'norm_g': norm_g, 'final_norm_g': final_norm_g,
            'ffn_w_gate': ffn_w_gate, 'ffn_w_up': ffn_w_up, 'ffn_w_down': ffn_w_down,
            's5_lam_re': s5_lam_re, 's5_lam_im': s5_lam_im, 's5_log_dt': s5_log_dt,
            's5_b_re': s5_b_re, 's5_b_im': s5_b_im, 's5_c_re': s5_c_re, 's5_c_im': s5_c_im,
            's5_d': s5_d, 's5_w_glu': s5_w_glu,
            'hg_w_in': hg_w_in, 'hg_lower_bounds': hg_lower_bounds, 'hg_norm_g': hg_norm_g,
            'hg_w_out': hg_w_out,
            'gdn_w_in': gdn_w_in, 'gdn_conv_w': gdn_conv_w, 'gdn_a_log': gdn_a_log,
            'gdn_dt_bias': gdn_dt_bias, 'gdn_norm_g': gdn_norm_g, 'gdn_w_out': gdn_w_out,
            'moba_w_in': moba_w_in, 'moba_w_out': moba_w_out, 'rel_bias_table': rel_bias_table}


def reference(x, norm_g, final_norm_g, ffn_w_gate, ffn_w_up, ffn_w_down,
              s5_lam_re, s5_lam_im, s5_log_dt, s5_b_re, s5_b_im, s5_c_re, s5_c_im, s5_d, s5_w_glu,
              hg_w_in, hg_lower_bounds, hg_norm_g, hg_w_out,
              gdn_w_in, gdn_conv_w, gdn_a_log, gdn_dt_bias, gdn_norm_g, gdn_w_out,
              moba_w_in, moba_w_out, rel_bias_table):
    lbs = jnp.cumsum(jax.nn.softmax(hg_lower_bounds.astype(jnp.float32), axis=0), axis=0)
    lbs = lbs - lbs[0]
    for i in range(DEPTH):
        m, j = i % N_MIXERS, i // N_MIXERS
        x = x + 0.5 * _swiglu(_rms(x, norm_g[i, 0]), ffn_w_gate[i, 0], ffn_w_up[i, 0], ffn_w_down[i, 0])
        h = _rms(x, norm_g[i, 1])
        if m == 0:
            y = _s5_mixer(h, s5_lam_re[j], s5_lam_im[j], s5_log_dt[j], s5_b_re[j], s5_b_im[j],
                          s5_c_re[j], s5_c_im[j], s5_d[j], s5_w_glu[j])
        elif m == 1:
            y = _hgrn2_mixer(h, hg_w_in[j], lbs[i], hg_norm_g[j], hg_w_out[j])
        elif m == 2:
            y = _gdn_mixer(h, gdn_w_in[j], gdn_conv_w[j], gdn_a_log[j], gdn_dt_bias[j],
                           gdn_norm_g[j], gdn_w_out[j])
        else:
            y = _moba_mixer(h, moba_w_in[j], rel_bias_table, moba_w_out[j])
        x = x + y.astype(x.dtype)
        x = x + 0.5 * _swiglu(_rms(x, norm_g[i, 2]), ffn_w_gate[i, 1], ffn_w_up[i, 1], ffn_w_down[i, 1])
    return _rms(x, final_norm_g)
```

```python
import functools
import math

import jax
import jax.numpy as jnp
import numpy as np
from jax import lax
from jax.experimental import pallas as pl
from jax.experimental.pallas import tpu as pltpu

F32 = jnp.float32
BF16 = jnp.bfloat16
LANES = 128
RMS_EPS = 1e-6

HG_HEADS = 8
HG_CHUNK = 32
GDN_QK_HEADS = 8
GDN_V_HEADS = 16
GDN_CHUNK = 64
GDN_CONV = 4
MOBA_HEADS = 8
MOBA_BLOCK = 256
MOBA_TOPK = 3
REL_BUCKETS = 32
REL_MAX_DIST = 2048
S5_GROUP = 16
S5_STATE = 64
S5_TC = 16
S5_GP = LANES // S5_GROUP

VMEM_LIMIT = 56 * 1024 * 1024


def _cparams(sem):
    return pltpu.CompilerParams(dimension_semantics=sem, vmem_limit_bytes=VMEM_LIMIT)


def _mm(a, b):
    return jnp.dot(a.astype(BF16), b.astype(BF16), preferred_element_type=F32)


def _mm_nt(a, b):
    return lax.dot_general(a.astype(BF16), b.astype(BF16), (((1,), (1,)), ((), ())),
                           preferred_element_type=F32)


def _mm_tn(a, b):
    return lax.dot_general(a.astype(BF16), b.astype(BF16), (((0,), (0,)), ((), ())),
                           preferred_element_type=F32)


def _mm_hi(a, b):
    return jnp.dot(a, b, precision=lax.Precision.HIGHEST, preferred_element_type=F32)


def _mm_nt_hi(a, b):
    return lax.dot_general(a, b, (((1,), (1,)), ((), ())), precision=lax.Precision.HIGHEST,
                           preferred_element_type=F32)


def _sigmoid(x):
    return 1.0 / (1.0 + jnp.exp(-x))


def _silu(x):
    return x * _sigmoid(x)


def _rms_rows(x, g):
    return x * lax.rsqrt(jnp.mean(x * x, axis=-1, keepdims=True) + RMS_EPS) * g


def _ffn_kernel(x_ref, g_ref, wg_ref, wu_ref, wd_ref, o_ref, h_ref, acc_ref):
    j = pl.program_id(1)

    @pl.when(j == 0)
    def _():
        h_ref[...] = _rms_rows(x_ref[...], g_ref[...]).astype(BF16)
        acc_ref[...] = jnp.zeros_like(acc_ref)

    h = h_ref[...]
    gate = jnp.dot(h, wg_ref[...].astype(BF16), preferred_element_type=F32)
    up = jnp.dot(h, wu_ref[...].astype(BF16), preferred_element_type=F32)
    a = (_silu(gate) * up).astype(BF16)
    acc_ref[...] += jnp.dot(a, wd_ref[...].astype(BF16), preferred_element_type=F32)

    @pl.when(j == pl.num_programs(1) - 1)
    def _():
        o_ref[...] = x_ref[...] + 0.5 * acc_ref[...]


def _ffn(x, g, wg, wu, wd):
    T, D = x.shape
    F = wg.shape[1]
    tm = min(1024, T)
    tf = 256
    return pl.pallas_call(
        _ffn_kernel,
        out_shape=jax.ShapeDtypeStruct((T, D), F32),
        grid=(T // tm, F // tf),
        in_specs=[pl.BlockSpec((tm, D), lambda i, j: (i, 0)),
                  pl.BlockSpec((1, D), lambda i, j: (0, 0)),
                  pl.BlockSpec((D, tf), lambda i, j: (0, j)),
                  pl.BlockSpec((D, tf), lambda i, j: (0, j)),
                  pl.BlockSpec((tf, D), lambda i, j: (j, 0))],
        out_specs=pl.BlockSpec((tm, D), lambda i, j: (i, 0)),
        scratch_shapes=[pltpu.VMEM((tm, D), BF16), pltpu.VMEM((tm, D), F32)],
        compiler_params=_cparams(("parallel", "arbitrary")),
        name="ffn",
    )(x, g.reshape(1, D), wg, wu, wd)


def _rms_kernel(x_ref, g_ref, o_ref):
    o_ref[...] = _rms_rows(x_ref[...], g_ref[...])


def _rms(x, g):
    T, D = x.shape
    tm = min(1024, T)
    return pl.pallas_call(
        _rms_kernel,
        out_shape=jax.ShapeDtypeStruct((T, D), F32),
        grid=(T // tm,),
        in_specs=[pl.BlockSpec((tm, D), lambda i: (i, 0)), pl.BlockSpec((1, D), lambda i: (0, 0))],
        out_specs=pl.BlockSpec((tm, D), lambda i: (i, 0)),
        compiler_params=_cparams(("parallel",)),
        name="rms",
    )(x, g.reshape(1, D))


def _rmsmm_kernel(x_ref, g_ref, w_ref, o_ref, h_ref):
    @pl.when(pl.program_id(1) == 0)
    def _():
        h_ref[...] = _rms_rows(x_ref[...], g_ref[...]).astype(BF16)

    o_ref[...] = jnp.dot(h_ref[...], w_ref[...].astype(BF16), preferred_element_type=F32)


def _rms_matmul(x, g, w, tn):
    T, D = x.shape
    N = w.shape[1]
    tm = min(1024, T)
    return pl.pallas_call(
        _rmsmm_kernel,
        out_shape=jax.ShapeDtypeStruct((T, N), F32),
        grid=(T // tm, N // tn),
        in_specs=[pl.BlockSpec((tm, D), lambda i, j: (i, 0)),
                  pl.BlockSpec((1, D), lambda i, j: (0, 0)),
                  pl.BlockSpec((D, tn), lambda i, j: (0, j))],
        out_specs=pl.BlockSpec((tm, tn), lambda i, j: (i, j)),
        scratch_shapes=[pltpu.VMEM((tm, D), BF16)],
        compiler_params=_cparams(("parallel", "arbitrary")),
        name="rms_matmul",
    )(x, g.reshape(1, D), w)


def _mmres_kernel(a_ref, w_ref, x_ref, o_ref):
    o_ref[...] = x_ref[...] + _mm(a_ref[...], w_ref[...])


def _matmul_residual(a, w, x):
    T, K = a.shape
    N = w.shape[1]
    tm = min(512, T)
    return pl.pallas_call(
        _mmres_kernel,
        out_shape=jax.ShapeDtypeStruct((T, N), F32),
        grid=(T // tm,),
        in_specs=[pl.BlockSpec((tm, K), lambda i: (i, 0)),
                  pl.BlockSpec((K, N), lambda i: (0, 0)),
                  pl.BlockSpec((tm, N), lambda i: (i, 0))],
        out_specs=pl.BlockSpec((tm, N), lambda i: (i, 0)),
        compiler_params=_cparams(("parallel",)),
        name="matmul_residual",
    )(a, w, x)


def _cmul(ar, ai, br, bi):
    return ar * br - ai * bi, ar * bi + ai * br


def _s5_powers(lr, li, ldt, n):
    lr = jnp.minimum(lr, -1e-4)
    dt = jnp.exp(ldt)
    mag = jnp.exp(lr * dt)
    br = mag * jnp.cos(li * dt)
    bi = mag * jnp.sin(li * dt)
    den = lr * lr + li * li
    fr = ((br - 1.0) * lr + bi * li) / den
    fi = (bi * lr - (br - 1.0) * li) / den
    pw = [(jnp.ones_like(br), jnp.zeros_like(bi))]
    for _ in range(n):
        pw.append(_cmul(pw[-1][0], pw[-1][1], br, bi))
    return pw, (fr, fi)


def _s5_prep_kernel(lr_row, li_row, ldt_row, lr_col, li_col, ldt_col, btr_ref, bti_ref, ctr_ref,
                    cti_ref, bd_ref, bexp_ref, cexp_ref, a_ref):
    tc = S5_TC
    ns = lr_row.shape[-1]
    pw_row, (fr, fi) = _s5_powers(lr_row[...], li_row[...], ldt_row[...], tc)
    pw_col, _ = _s5_powers(lr_col[...], li_col[...], ldt_col[...], tc)
    btr, bti = btr_ref[...], bti_ref[...]
    ctr, cti = ctr_ref[...], cti_ref[...]
    for k in range(tc):
        fer, fei = _cmul(fr, fi, pw_row[k][0], pw_row[k][1])
        xr = btr * fer - bti * fei
        xi = btr * fei + bti * fer
        bd_ref[k] = (_mm_hi(xr, ctr) - _mm_hi(xi, cti)).astype(BF16)
        s = tc - 1 - k
        bexp_ref[s * LANES:(s + 1) * LANES, 0:ns] = xr.astype(BF16)
        bexp_ref[s * LANES:(s + 1) * LANES, ns:2 * ns] = xi.astype(BF16)
        er, ei = pw_col[k + 1]
        cexp_ref[0:ns, k * LANES:(k + 1) * LANES] = (ctr * er - cti * ei).astype(BF16)
        cexp_ref[ns:2 * ns, k * LANES:(k + 1) * LANES] = (-(ctr * ei + cti * er)).astype(BF16)
    a_ref[:, 0:ns] = pw_row[tc][0]
    a_ref[:, ns:2 * ns] = pw_row[tc][1]


def _s5_main_kernel(u_ref, bd_ref, bexp_ref, cexp_ref, a_ref, y_ref, m_ref, hl_ref, hp_ref):
    tc = S5_TC
    ns = a_ref.shape[-1] // 2
    rows = u_ref.shape[0]

    @pl.when(pl.program_id(1) == 0)
    def _():
        for s in range(tc):
            for t in range(tc):
                blk = bd_ref[t - s] if t >= s else jnp.zeros((LANES, LANES), BF16)
                m_ref[s * LANES:(s + 1) * LANES, t * LANES:(t + 1) * LANES] = blk

    u = u_ref[...].astype(BF16)
    hl_ref[...] = jnp.dot(u, bexp_ref[...], preferred_element_type=F32)
    are, aim = a_ref[:, 0:ns], a_ref[:, ns:2 * ns]

    def step(c, carry):
        hre, him = carry
        hp_ref[pl.ds(c, 1), 0:ns] = hre
        hp_ref[pl.ds(c, 1), ns:2 * ns] = him
        xre = hl_ref[pl.ds(c, 1), 0:ns]
        xim = hl_ref[pl.ds(c, 1), ns:2 * ns]
        return are * hre - aim * him + xre, are * him + aim * hre + xim

    z = jnp.zeros((1, ns), F32)
    lax.fori_loop(0, rows, step, (z, z))
    y_ref[...] = (jnp.dot(u, m_ref[...], preferred_element_type=F32)
                  + jnp.dot(hp_ref[...].astype(BF16), cexp_ref[...], preferred_element_type=F32))


def _s5_post_kernel(y_ref, u_ref, d_ref, wv_ref, wg_ref, x_ref, o_ref, a_ref):
    @pl.when(pl.program_id(1) == 0)
    def _():
        y = y_ref[...] + d_ref[...] * u_ref[...]
        c = math.sqrt(2.0 / math.pi)
        a_ref[...] = (0.5 * y * (1.0 + jnp.tanh(c * (y + 0.044715 * (y * y * y))))).astype(BF16)

    a = a_ref[...]
    val = jnp.dot(a, wv_ref[...].astype(BF16), preferred_element_type=F32)
    gate = jnp.dot(a, wg_ref[...].astype(BF16), preferred_element_type=F32)
    o_ref[...] = x_ref[...] + val * _sigmoid(gate)


def _s5_mixer(x, g, lam_re, lam_im, log_dt, b_re, b_im, c_re, c_im, d_skip, w_glu, bsz):
    T, D = x.shape
    L = T // bsz
    G, P = lam_re.shape
    tc, gp = S5_TC, S5_GP
    ng = G // gp
    ns = gp * P
    rows = L // tc
    u = _rms(x, g)

    eye = jnp.eye(gp, dtype=F32)

    def bt(b):
        return jnp.einsum('ngph,gk->nghkp', b.reshape(ng, gp, P, S5_GROUP), eye).reshape(ng, LANES, ns)

    def ct(c):
        return jnp.einsum('nghp,gk->ngpkh', c.reshape(ng, gp, S5_GROUP, P), eye).reshape(ng, ns, LANES)

    lr, li = lam_re.reshape(ng, 1, ns), lam_im.reshape(ng, 1, ns)
    ldt = jnp.broadcast_to(log_dt[:, None], (G, P)).reshape(ng, 1, ns)
    row_spec = pl.BlockSpec((None, 1, ns), lambda n: (n, 0, 0))
    col_spec = pl.BlockSpec((None, ns, 1), lambda n: (n, 0, 0))
    bd, bexp, cexp, a16 = pl.pallas_call(
        _s5_prep_kernel,
        out_shape=(jax.ShapeDtypeStruct((ng, tc, LANES, LANES), BF16),
                   jax.ShapeDtypeStruct((ng, tc * LANES, 2 * ns), BF16),
                   jax.ShapeDtypeStruct((ng, 2 * ns, tc * LANES), BF16),
                   jax.ShapeDtypeStruct((ng, 1, 2 * ns), F32)),
        grid=(ng,),
        in_specs=[row_spec, row_spec, row_spec, col_spec, col_spec, col_spec,
                  pl.BlockSpec((None, LANES, ns), lambda n: (n, 0, 0)),
                  pl.BlockSpec((None, LANES, ns), lambda n: (n, 0, 0)),
                  pl.BlockSpec((None, ns, LANES), lambda n: (n, 0, 0)),
                  pl.BlockSpec((None, ns, LANES), lambda n: (n, 0, 0))],
        out_specs=(pl.BlockSpec((None, tc, LANES, LANES), lambda n: (n, 0, 0, 0)),
                   pl.BlockSpec((None, tc * LANES, 2 * ns), lambda n: (n, 0, 0)),
                   pl.BlockSpec((None, 2 * ns, tc * LANES), lambda n: (n, 0, 0)),
                   pl.BlockSpec((None, 1, 2 * ns), lambda n: (n, 0, 0))),
        compiler_params=_cparams(("parallel",)),
        name="s5_prep",
    )(lr, li, ldt, lr.reshape(ng, ns, 1), li.reshape(ng, ns, 1), ldt.reshape(ng, ns, 1),
      bt(b_re), bt(b_im), ct(c_re), ct(c_im))

    ug = u.reshape(bsz, rows, tc, ng, LANES).transpose(3, 0, 1, 2, 4).reshape(ng, bsz, rows, tc * LANES)
    yg = pl.pallas_call(
        _s5_main_kernel,
        out_shape=jax.ShapeDtypeStruct((ng, bsz, rows, tc * LANES), F32),
        grid=(ng, bsz),
        in_specs=[pl.BlockSpec((None, None, rows, tc * LANES), lambda n, b: (n, b, 0, 0)),
                  pl.BlockSpec((None, tc, LANES, LANES), lambda n, b: (n, 0, 0, 0)),
                  pl.BlockSpec((None, tc * LANES, 2 * ns), lambda n, b: (n, 0, 0)),
                  pl.BlockSpec((None, 2 * ns, tc * LANES), lambda n, b: (n, 0, 0)),
                  pl.BlockSpec((None, 1, 2 * ns), lambda n, b: (n, 0, 0))],
        out_specs=pl.BlockSpec((None, None, rows, tc * LANES), lambda n, b: (n, b, 0, 0)),
        scratch_shapes=[pltpu.VMEM((tc * LANES, tc * LANES), BF16),
                        pltpu.VMEM((rows, 2 * ns), F32),
                        pltpu.VMEM((rows, 2 * ns), F32)],
        compiler_params=_cparams(("parallel", "arbitrary")),
        name="s5_main",
    )(ug, bd, bexp, cexp, a16)
    y = yg.reshape(ng, bsz, rows, tc, LANES).transpose(1, 2, 3, 0, 4).reshape(T, D)

    tm = min(1024, T)
    tn = 512
    nj = D // tn
    return pl.pallas_call(
        _s5_post_kernel,
        out_shape=jax.ShapeDtypeStruct((T, D), F32),
        grid=(T // tm, nj),
        in_specs=[pl.BlockSpec((tm, D), lambda i, j: (i, 0)),
                  pl.BlockSpec((tm, D), lambda i, j: (i, 0)),
                  pl.BlockSpec((1, D), lambda i, j: (0, 0)),
                  pl.BlockSpec((D, tn), lambda i, j: (0, j)),
                  pl.BlockSpec((D, tn), lambda i, j: (0, j + nj)),
                  pl.BlockSpec((tm, tn), lambda i, j: (i, j))],
        out_specs=pl.BlockSpec((tm, tn), lambda i, j: (i, j)),
        scratch_shapes=[pltpu.VMEM((tm, D), BF16)],
        compiler_params=_cparams(("parallel", "arbitrary")),
        name="s5_post",
    )(y, u, d_skip.reshape(1, D), w_glu, w_glu, x)


HG_SUPER = 256


def _hgrn_kernel(q_ref, f_ref, i_ref, g_ref, lb_ref, ng_ref, o_ref, st_ref, *, layer_idx):
    L, dk = q_ref.shape
    C, SC = HG_CHUNK, HG_SUPER
    lbraw = lb_ref[...]
    e = jnp.exp(lbraw - jnp.max(lbraw, axis=0, keepdims=True))
    sm = e / jnp.sum(e, axis=0, keepdims=True)
    layer = lax.broadcasted_iota(jnp.int32, sm.shape, 0)
    lb = jnp.sum(jnp.where((layer >= 1) & (layer <= layer_idx), sm, 0.0), axis=0, keepdims=True)

    ti = lax.broadcasted_iota(jnp.int32, (SC, SC), 0)
    si = lax.broadcasted_iota(jnp.int32, (SC, SC), 1)
    same = (ti // C) == (si // C)
    causal = same & (si <= ti)
    tri_f = jnp.where(causal, 1.0, 0.0).astype(F32)
    same_f = jnp.where(same, 1.0, 0.0).astype(F32)
    ng = ng_ref[...]
    st_ref[...] = jnp.zeros_like(st_ref)

    def body(sc, _):
        r0 = pl.multiple_of(sc * SC, SC)
        f = f_ref[pl.ds(r0, SC), :]
        fgate = lb + (1.0 - lb) * _sigmoid(f)
        logf = jnp.log(fgate)
        k = (1.0 - lb) * _sigmoid(-f)
        q = _silu(q_ref[pl.ds(r0, SC), :]) * dk ** -0.5
        v = i_ref[pl.ds(r0, SC), :]
        b = _mm_hi(tri_f, logf)
        blast = _mm_hi(same_f, logf)
        q_t = q * jnp.exp(b)
        k_t = k * jnp.exp(-b)
        k_state = k * jnp.exp(blast - b)
        attn = jnp.where(causal, _mm_nt(q_t, k_t), 0.0)
        o = _mm(attn, v)
        inter = []
        for c in range(SC // C):
            sl = slice(c * C, (c + 1) * C)
            st = st_ref[...]
            inter.append(_mm_nt(q_t[sl], st))
            dc = jnp.exp(blast[c * C:c * C + 1, :])
            st_ref[...] = st * dc + _mm_tn(v[sl], k_state[sl])
        o = o + jnp.concatenate(inter, axis=0)
        o = _rms_rows(o, ng) * _silu(g_ref[pl.ds(r0, SC), :])
        o_ref[pl.ds(r0, SC), :] = o
        return 0

    lax.fori_loop(0, L // SC, body, 0)


def _hgrn_mixer(x, g, w_in, lower_bounds, norm_g, w_out, bsz, layer_idx):
    T, D = x.shape
    L = T // bsz
    H = HG_HEADS
    dk = D // H
    proj = _rms_matmul(x, g, w_in, 512).reshape(bsz, L, 4 * D)
    nl = lower_bounds.shape[0]

    def col(off):
        return pl.BlockSpec((None, L, dk), lambda b, h: (b, 0, off * H + h))

    o = pl.pallas_call(
        functools.partial(_hgrn_kernel, layer_idx=layer_idx),
        out_shape=jax.ShapeDtypeStruct((bsz, L, D), F32),
        grid=(bsz, H),
        in_specs=[col(0), col(1), col(2), col(3),
                  pl.BlockSpec((nl, dk), lambda b, h: (0, h)),
                  pl.BlockSpec((1, dk), lambda b, h: (0, 0))],
        out_specs=pl.BlockSpec((None, L, dk), lambda b, h: (b, 0, h)),
        scratch_shapes=[pltpu.VMEM((dk, dk), F32)],
        compiler_params=_cparams(("parallel", "parallel")),
        name="hgrn",
    )(proj, proj, proj, proj, lower_bounds, norm_g.reshape(1, dk))
    return _matmul_residual(o.reshape(T, D), w_out, x)


GDN_LTILE = 512


def _gdn_kernel(q_ref, k_ref, v_ref, z_ref, gc_ref, cwq_ref, cwk_ref, cwv_ref, alog_ref, dtb_ref,
                ng_ref, o_ref, xq_ref, xk_ref, xv_ref, qn_ref, kn_ref, vc_ref, bcol_ref, gcol_ref,
                s_ref):
    Lt, dk = q_ref.shape
    C = GDN_CHUNK
    hk = pl.program_id(1)
    first = pl.program_id(2) == 0
    nvh = GDN_V_HEADS // GDN_QK_HEADS
    pad = 8

    @pl.when(first)
    def _():
        s_ref[...] = jnp.zeros_like(s_ref)
        xq_ref[0:pad, :] = jnp.zeros((pad, xq_ref.shape[1]), F32)
        xk_ref[0:pad, :] = jnp.zeros((pad, xk_ref.shape[1]), F32)
        xv_ref[0:pad, :] = jnp.zeros((pad, xv_ref.shape[1]), F32)

    def conv_silu(src_ref, w_ref, xp_ref):
        xp_ref[pad:pad + Lt, :] = src_ref[...]
        w = w_ref[...]
        acc = None
        for j in range(GDN_CONV):
            s = pad - (GDN_CONV - 1) + j
            term = w[j:j + 1, :] * xp_ref[s:s + Lt, :]
            acc = term if acc is None else acc + term
        xp_ref[0:pad, :] = xp_ref[Lt:Lt + pad, :]
        return _silu(acc)

    def l2n(t):
        return t * lax.rsqrt(jnp.sum(t * t, axis=-1, keepdims=True) + 1e-6)

    qn_ref[...] = l2n(conv_silu(q_ref, cwq_ref, xq_ref)) * dk ** -0.5
    kn_ref[...] = l2n(conv_silu(k_ref, cwk_ref, xk_ref))
    vc_ref[...] = conv_silu(v_ref, cwv_ref, xv_ref)

    gcraw = gc_ref[...]
    bcol_ref[...] = _sigmoid(gcraw)
    xs = gcraw + dtb_ref[...]
    softplus = jnp.maximum(xs, 0.0) + jnp.log(1.0 + jnp.exp(-jnp.abs(xs)))
    gcol_ref[...] = -jnp.exp(alog_ref[...]) * softplus

    ti = lax.broadcasted_iota(jnp.int32, (C, C), 0)
    si = lax.broadcasted_iota(jnp.int32, (C, C), 1)
    incl = si <= ti
    strict = si < ti
    tri_f = jnp.where(incl, 1.0, 0.0).astype(F32)
    lane = lax.broadcasted_iota(jnp.int32, (C, LANES), 1)
    pick0 = jnp.where(lane == 0, 1.0, 0.0).astype(F32)
    ng = ng_ref[...]

    def body(c, _):
        r0 = pl.multiple_of(c * C, C)
        qc = qn_ref[pl.ds(r0, C), :]
        kc = kn_ref[pl.ds(r0, C), :]
        kk = _mm_nt(kc, kc)
        qk0 = _mm_nt(qc, kc)
        braw = bcol_ref[pl.ds(r0, C), :]
        graw = gcol_ref[pl.ds(r0, C), :]
        for j in range(nvh):
            hv = nvh * hk + j
            beta = jnp.sum(jnp.where(lane == hv, braw, 0.0), axis=1, keepdims=True)
            gdec = jnp.sum(jnp.where(lane == hv + GDN_V_HEADS, graw, 0.0), axis=1, keepdims=True)
            bcum = _mm_hi(tri_f, jnp.broadcast_to(gdec, (C, LANES)))
            brow = _mm_nt_hi(pick0, bcum)
            dmat = bcum[:, 0:C] - brow
            decay = jnp.where(incl, jnp.exp(jnp.where(incl, dmat, 0.0)), 0.0)
            vj = vc_ref[pl.ds(r0, C), j * dk:(j + 1) * dk]
            eb = jnp.exp(bcum)
            kb = kc * beta
            pmat = jnp.where(strict, -(kk * beta) * decay, 0.0)
            sol = jnp.concatenate([vj * beta, kb * eb], axis=1)
            for it in range(6):
                sol = sol + _mm_hi(pmat, sol)
                if it < 5:
                    pmat = _mm_hi(pmat, pmat)
            u_val = sol[:, 0:dk]
            w_cum = sol[:, dk:2 * dk]
            qk = qk0 * decay
            q_dec = qc * eb
            blast = bcum[C - 1:C, :]
            k_state = kc * jnp.exp(blast - bcum)
            S = s_ref[j]
            v_new = u_val - _mm(w_cum, S)
            o = _mm(q_dec, S) + _mm(qk, v_new)
            s_ref[j] = S * jnp.exp(blast) + _mm_tn(k_state, v_new)
            z = z_ref[pl.ds(r0, C), j * dk:(j + 1) * dk]
            o_ref[pl.ds(r0, C), j * dk:(j + 1) * dk] = _rms_rows(o, ng) * _silu(z)
        return 0

    lax.fori_loop(0, Lt // C, body, 0)


def _gdn_mixer(x, g, w_in, conv_w, a_log, dt_bias, norm_g, w_out, bsz):
    T, D = x.shape
    L = T // bsz
    Hk, Hv = GDN_QK_HEADS, GDN_V_HEADS
    dk = LANES
    nvh = Hv // Hk
    lt = min(GDN_LTILE, L)
    n_in = w_in.shape[1]
    n_pad = -(-n_in // (7 * LANES)) * (7 * LANES)
    w_pad = jnp.pad(w_in, ((0, 0), (0, n_pad - n_in)))
    proj = _rms_matmul(x, g, w_pad, 7 * LANES).reshape(bsz, L, n_pad)
    qkv_w = 2 * Hk * dk + Hv * dk
    gate_blk = (qkv_w + Hv * dk) // LANES
    alog_row = jnp.zeros((1, LANES), F32).at[0, Hv:2 * Hv].set(a_log)
    dtb_row = jnp.zeros((1, LANES), F32).at[0, Hv:2 * Hv].set(dt_bias)
    o = pl.pallas_call(
        _gdn_kernel,
        out_shape=jax.ShapeDtypeStruct((bsz, L, Hv * dk), F32),
        grid=(bsz, Hk, L // lt),
        in_specs=[pl.BlockSpec((None, lt, dk), lambda b, h, t: (b, t, h)),
                  pl.BlockSpec((None, lt, dk), lambda b, h, t: (b, t, Hk + h)),
                  pl.BlockSpec((None, lt, nvh * dk), lambda b, h, t: (b, t, Hk + h)),
                  pl.BlockSpec((None, lt, nvh * dk), lambda b, h, t: (b, t, 2 * Hk + h)),
                  pl.BlockSpec((None, lt, LANES), lambda b, h, t: (b, t, gate_blk)),
                  pl.BlockSpec((GDN_CONV, dk), lambda b, h, t: (0, h)),
                  pl.BlockSpec((GDN_CONV, dk), lambda b, h, t: (0, Hk + h)),
                  pl.BlockSpec((GDN_CONV, nvh * dk), lambda b, h, t: (0, Hk + h)),
                  pl.BlockSpec((1, LANES), lambda b, h, t: (0, 0)),
                  pl.BlockSpec((1, LANES), lambda b, h, t: (0, 0)),
                  pl.BlockSpec((1, dk), lambda b, h, t: (0, 0))],
        out_specs=pl.BlockSpec((None, lt, nvh * dk), lambda b, h, t: (b, t, h)),
        scratch_shapes=[pltpu.VMEM((lt + 8, dk), F32),
                        pltpu.VMEM((lt + 8, dk), F32),
                        pltpu.VMEM((lt + 8, nvh * dk), F32),
                        pltpu.VMEM((lt, dk), F32),
                        pltpu.VMEM((lt, dk), F32),
                        pltpu.VMEM((lt, nvh * dk), F32),
                        pltpu.VMEM((lt, LANES), F32),
                        pltpu.VMEM((lt, LANES), F32),
                        pltpu.VMEM((nvh, dk, dk), F32)],
        compiler_params=_cparams(("parallel", "parallel", "arbitrary")),
        name="gdn",
    )(proj, proj, proj, proj, proj, conv_w, conv_w, conv_w, alog_row, dtb_row, norm_g.reshape(1, dk))
    return _matmul_residual(o.reshape(T, Hv * dk), w_out, x)


def _t5_bucket_np(n):
    max_exact = REL_BUCKETS // 2
    nf = np.maximum(n, 1).astype(np.float32)
    large = max_exact + (np.log(nf / np.float32(max_exact)) / np.float32(math.log(REL_MAX_DIST / max_exact))
                         * np.float32(REL_BUCKETS - max_exact)).astype(np.int32)
    large = np.minimum(large, REL_BUCKETS - 1)
    return np.where(n < max_exact, n, large).astype(np.int32)


def _bias_vec_kernel(bucket_ref, table_ref, o_ref):
    bucket = bucket_ref[...]
    table = table_ref[...]
    acc = jnp.zeros(o_ref.shape, F32)
    for k in range(REL_BUCKETS):
        acc = acc + jnp.where(bucket == k, table[:, k:k + 1], 0.0)
    o_ref[...] = acc


def _moba_kernel(q_ref, k_ref, v_ref, bias_ref, o_ref, kmean_ref, *, L):
    BS = MOBA_BLOCK
    dh = q_ref.shape[-1]
    nb = L // BS
    qb = pl.program_id(2)

    @pl.when(qb == 0)
    def _():
        kmean_ref[...] = jnp.zeros_like(kmean_ref)
        for n in range(nb):
            kmean_ref[n:n + 1, :] = jnp.sum(k_ref[n * BS:(n + 1) * BS, :], axis=0, keepdims=True) / BS

    q = q_ref[...]
    lane = lax.broadcasted_iota(jnp.int32, (BS, LANES), 1)
    past = lane < qb
    gate = jnp.where(past, _mm_nt_hi(q, kmean_ref[...]), -1e30)
    sel = jnp.zeros((BS, LANES), F32)
    for _ in range(MOBA_TOPK):
        mx = jnp.max(gate, axis=1, keepdims=True)
        first = jnp.min(jnp.where(gate == mx, lane, LANES), axis=1, keepdims=True)
        pick = lane == first
        sel = jnp.where(pick & past, 1.0, sel)
        gate = jnp.where(pick, -jnp.inf, gate)

    scale = dh ** -0.5
    rev_len = bias_ref.shape[-1]

    def bias_tile(d):
        start = pl.multiple_of(rev_len - 2 * BS - d * BS, BS)
        w = bias_ref[:, pl.ds(start, 2 * BS)]
        wb = jnp.broadcast_to(w, (BS, 2 * BS))
        return pltpu.roll(wb, BS + 1, 1, stride=1, stride_axis=0)[:, 0:BS]

    qi = lax.broadcasted_iota(jnp.int32, (BS, BS), 0)
    ki = lax.broadcasted_iota(jnp.int32, (BS, BS), 1)
    own0 = pl.multiple_of(qb * BS, BS)
    s = _mm_nt(q, k_ref[pl.ds(own0, BS), :]) * scale + bias_tile(0)
    s = jnp.where(ki <= qi, s, -jnp.inf)
    m = jnp.max(s, axis=1, keepdims=True)
    p = jnp.exp(s - m)
    l = jnp.sum(p, axis=1, keepdims=True)
    acc = _mm(p, v_ref[pl.ds(own0, BS), :])

    def body(j, carry):
        m, l, acc = carry
        r0 = pl.multiple_of(j * BS, BS)
        s = _mm_nt(q, k_ref[pl.ds(r0, BS), :]) * scale + bias_tile(qb - j)
        chosen = jnp.sum(jnp.where(lane == j, sel, 0.0), axis=1, keepdims=True) > 0.5
        s = jnp.where(chosen, s, -jnp.inf)
        m_new = jnp.maximum(m, jnp.max(s, axis=1, keepdims=True))
        alpha = jnp.exp(m - m_new)
        p = jnp.exp(s - m_new)
        l = alpha * l + jnp.sum(p, axis=1, keepdims=True)
        acc = alpha * acc + _mm(p, v_ref[pl.ds(r0, BS), :])
        return m_new, l, acc

    m, l, acc = lax.fori_loop(0, qb, body, (m, l, acc))
    o_ref[...] = acc / l


def _moba_mixer(x, g, w_in, rel_table, w_out, bsz):
    T, D = x.shape
    L = T // bsz
    H, dh, BS = MOBA_HEADS, LANES, MOBA_BLOCK
    nb = L // BS
    proj = _rms_matmul(x, g, w_in, 512).reshape(bsz, L, 3 * D)
    rev_len = L + BS
    dist = np.maximum(L - 1 - np.arange(rev_len), 0)
    bucket_rev = jnp.asarray(_t5_bucket_np(dist)[None, :])
    bias_rev = pl.pallas_call(
        _bias_vec_kernel,
        out_shape=jax.ShapeDtypeStruct((H, rev_len), F32),
        name="moba_bias",
    )(bucket_rev, rel_table.T).reshape(H, 1, rev_len)
    o = pl.pallas_call(
        functools.partial(_moba_kernel, L=L),
        out_shape=jax.ShapeDtypeStruct((bsz, L, D), F32),
        grid=(bsz, H, nb),
        in_specs=[pl.BlockSpec((None, BS, dh), lambda b, h, n: (b, n, h)),
                  pl.BlockSpec((None, L, dh), lambda b, h, n: (b, 0, H + h)),
                  pl.BlockSpec((None, L, dh), lambda b, h, n: (b, 0, 2 * H + h)),
                  pl.BlockSpec((None, 1, rev_len), lambda b, h, n: (h, 0, 0))],
        out_specs=pl.BlockSpec((None, BS, dh), lambda b, h, n: (b, n, h)),
        scratch_shapes=[pltpu.VMEM((LANES, dh), F32)],
        compiler_params=_cparams(("parallel", "parallel", "arbitrary")),
        name="moba",
    )(proj, proj, proj, bias_rev)
    return _matmul_residual(o.reshape(T, D), w_out, x)


def kernel(x, norm_g, final_norm_g, ffn_w_gate, ffn_w_up, ffn_w_down, s5_lam_re, s5_lam_im, s5_log_dt, s5_b_re, s5_b_im, s5_c_re, s5_c_im, s5_d, s5_w_glu, hg_w_in, hg_lower_bounds, hg_norm_g, hg_w_out, gdn_w_in, gdn_conv_w, gdn_a_log, gdn_dt_bias, gdn_norm_g, gdn_w_out, moba_w_in, moba_w_out, rel_bias_table):
    bsz, L, D = x.shape
    depth = norm_g.shape[0]
    h = x.reshape(bsz * L, D)
    for i in range(depth):
        m, j = i % 4, i // 4
        h = _ffn(h, norm_g[i, 0], ffn_w_gate[i, 0], ffn_w_up[i, 0], ffn_w_down[i, 0])
        if m == 0:
            h = _s5_mixer(h, norm_g[i, 1], s5_lam_re[j], s5_lam_im[j], s5_log_dt[j], s5_b_re[j],
                          s5_b_im[j], s5_c_re[j], s5_c_im[j], s5_d[j], s5_w_glu[j], bsz)
        elif m == 1:
            h = _hgrn_mixer(h, norm_g[i, 1], hg_w_in[j], hg_lower_bounds, hg_norm_g[j], hg_w_out[j],
                            bsz, i)
        elif m == 2:
            h = _gdn_mixer(h, norm_g[i, 1], gdn_w_in[j], gdn_conv_w[j], gdn_a_log[j], gdn_dt_bias[j],
                           gdn_norm_g[j], gdn_w_out[j], bsz)
        else:
            h = _moba_mixer(h, norm_g[i, 1], moba_w_in[j], rel_bias_table, moba_w_out[j], bsz)
        h = _ffn(h, norm_g[i, 2], ffn_w_gate[i, 1], ffn_w_up[i, 1], ffn_w_down[i, 1])
    return _rms(h, final_norm_g).reshape(bsz, L, D)
```

```python
import functools
import math

import jax
import jax.numpy as jnp
import numpy as np
from jax import lax
from jax.experimental import pallas as pl
from jax.experimental.pallas import tpu as pltpu

F32 = jnp.float32
BF16 = jnp.bfloat16
LANES = 128
RMS_EPS = 1e-6

HG_HEADS = 8
HG_CHUNK = 32
GDN_QK_HEADS = 8
GDN_V_HEADS = 16
GDN_CHUNK = 64
GDN_CONV = 4
MOBA_HEADS = 8
MOBA_BLOCK = 256
MOBA_TOPK = 3
REL_BUCKETS = 32
REL_MAX_DIST = 2048
S5_GROUP = 16
S5_STATE = 64
S5_TC = 16
S5_GP = LANES // S5_GROUP

VMEM_LIMIT = 56 * 1024 * 1024


def _cparams(sem):
    return pltpu.CompilerParams(dimension_semantics=sem, vmem_limit_bytes=VMEM_LIMIT)


def _mm(a, b):
    return jnp.dot(a.astype(BF16), b.astype(BF16), preferred_element_type=F32)


def _mm_nt(a, b):
    return lax.dot_general(a.astype(BF16), b.astype(BF16), (((1,), (1,)), ((), ())),
                           preferred_element_type=F32)


def _mm_tn(a, b):
    return lax.dot_general(a.astype(BF16), b.astype(BF16), (((0,), (0,)), ((), ())),
                           preferred_element_type=F32)


def _mm_hi(a, b):
    return jnp.dot(a, b, precision=lax.Precision.HIGHEST, preferred_element_type=F32)


def _mm_nt_hi(a, b):
    return lax.dot_general(a, b, (((1,), (1,)), ((), ())), precision=lax.Precision.HIGHEST,
                           preferred_element_type=F32)


def _sigmoid(x):
    return 1.0 / (1.0 + jnp.exp(-x))


def _silu(x):
    return x * _sigmoid(x)


def _rms_rows(x, g):
    return x * lax.rsqrt(jnp.mean(x * x, axis=-1, keepdims=True) + RMS_EPS) * g


def _ffn_kernel(x_ref, g_ref, wg_ref, wu_ref, wd_ref, o_ref, h_ref, acc_ref):
    j = pl.program_id(1)

    @pl.when(j == 0)
    def _():
        h_ref[...] = _rms_rows(x_ref[...], g_ref[...]).astype(BF16)
        acc_ref[...] = jnp.zeros_like(acc_ref)

    h = h_ref[...]
    gate = jnp.dot(h, wg_ref[...].astype(BF16), preferred_element_type=F32)
    up = jnp.dot(h, wu_ref[...].astype(BF16), preferred_element_type=F32)
    a = (_silu(gate) * up).astype(BF16)
    acc_ref[...] += jnp.dot(a, wd_ref[...].astype(BF16), preferred_element_type=F32)

    @pl.when(j == pl.num_programs(1) - 1)
    def _():
        o_ref[...] = x_ref[...] + 0.5 * acc_ref[...]


def _ffn(x, g, wg, wu, wd):
    T, D = x.shape
    F = wg.shape[1]
    tm = min(1024, T)
    tf = 256
    return pl.pallas_call(
        _ffn_kernel,
        out_shape=jax.ShapeDtypeStruct((T, D), F32),
        grid=(T // tm, F // tf),
        in_specs=[pl.BlockSpec((tm, D), lambda i, j: (i, 0)),
                  pl.BlockSpec((1, D), lambda i, j: (0, 0)),
                  pl.BlockSpec((D, tf), lambda i, j: (0, j)),
                  pl.BlockSpec((D, tf), lambda i, j: (0, j)),
                  pl.BlockSpec((tf, D), lambda i, j: (j, 0))],
        out_specs=pl.BlockSpec((tm, D), lambda i, j: (i, 0)),
        scratch_shapes=[pltpu.VMEM((tm, D), BF16), pltpu.VMEM((tm, D), F32)],
        compiler_params=_cparams(("parallel", "arbitrary")),
        name="ffn",
    )(x, g.reshape(1, D), wg, wu, wd)


def _rms_kernel(x_ref, g_ref, o_ref):
    o_ref[...] = _rms_rows(x_ref[...], g_ref[...])


def _rms(x, g):
    T, D = x.shape
    tm = min(1024, T)
    return pl.pallas_call(
        _rms_kernel,
        out_shape=jax.ShapeDtypeStruct((T, D), F32),
        grid=(T // tm,),
        in_specs=[pl.BlockSpec((tm, D), lambda i: (i, 0)), pl.BlockSpec((1, D), lambda i: (0, 0))],
        out_specs=pl.BlockSpec((tm, D), lambda i: (i, 0)),
        compiler_params=_cparams(("parallel",)),
        name="rms",
    )(x, g.reshape(1, D))


def _rmsmm_kernel(x_ref, g_ref, w_ref, o_ref, h_ref):
    @pl.when(pl.program_id(1) == 0)
    def _():
        h_ref[...] = _rms_rows(x_ref[...], g_ref[...]).astype(BF16)

    o_ref[...] = jnp.dot(h_ref[...], w_ref[...].astype(BF16), preferred_element_type=F32)


def _rms_matmul(x, g, w, tn):
    T, D = x.shape
    N = w.shape[1]
    tm = min(1024, T)
    return pl.pallas_call(
        _rmsmm_kernel,
        out_shape=jax.ShapeDtypeStruct((T, N), F32),
        grid=(T // tm, N // tn),
        in_specs=[pl.BlockSpec((tm, D), lambda i, j: (i, 0)),
                  pl.BlockSpec((1, D), lambda i, j: (0, 0)),
                  pl.BlockSpec((D, tn), lambda i, j: (0, j))],
        out_specs=pl.BlockSpec((tm, tn), lambda i, j: (i, j)),
        scratch_shapes=[pltpu.VMEM((tm, D), BF16)],
        compiler_params=_cparams(("parallel", "arbitrary")),
        name="rms_matmul",
    )(x, g.reshape(1, D), w)


def _mmres_kernel(a_ref, w_ref, x_ref, o_ref):
    o_ref[...] = x_ref[...] + _mm(a_ref[...], w_ref[...])


def _matmul_residual(a, w, x):
    T, K = a.shape
    N = w.shape[1]
    tm = min(512, T)
    return pl.pallas_call(
        _mmres_kernel,
        out_shape=jax.ShapeDtypeStruct((T, N), F32),
        grid=(T // tm,),
        in_specs=[pl.BlockSpec((tm, K), lambda i: (i, 0)),
                  pl.BlockSpec((K, N), lambda i: (0, 0)),
                  pl.BlockSpec((tm, N), lambda i: (i, 0))],
        out_specs=pl.BlockSpec((tm, N), lambda i: (i, 0)),
        compiler_params=_cparams(("parallel",)),
        name="matmul_residual",
    )(a, w, x)


def _cmul(ar, ai, br, bi):
    return ar * br - ai * bi, ar * bi + ai * br


def _s5_powers(lr, li, ldt, n):
    lr = jnp.minimum(lr, -1e-4)
    dt = jnp.exp(ldt)
    mag = jnp.exp(lr * dt)
    br = mag * jnp.cos(li * dt)
    bi = mag * jnp.sin(li * dt)
    den = lr * lr + li * li
    fr = ((br - 1.0) * lr + bi * li) / den
    fi = (bi * lr - (br - 1.0) * li) / den
    pw = [(jnp.ones_like(br), jnp.zeros_like(bi))]
    for _ in range(n):
        pw.append(_cmul(pw[-1][0], pw[-1][1], br, bi))
    return pw, (fr, fi)


def _s5_prep_kernel(lr_row, li_row, ldt_row, lr_col, li_col, ldt_col, btr_ref, bti_ref, ctr_ref,
                    cti_ref, bd_ref, bexp_ref, cexp_ref, a_ref):
    tc = S5_TC
    ns = lr_row.shape[-1]
    pw_row, (fr, fi) = _s5_powers(lr_row[...], li_row[...], ldt_row[...], tc)
    pw_col, _ = _s5_powers(lr_col[...], li_col[...], ldt_col[...], tc)
    btr, bti = btr_ref[...], bti_ref[...]
    ctr, cti = ctr_ref[...], cti_ref[...]
    for k in range(tc):
        fer, fei = _cmul(fr, fi, pw_row[k][0], pw_row[k][1])
        xr = btr * fer - bti * fei
        xi = btr * fei + bti * fer
        bd_ref[k] = (_mm_hi(xr, ctr) - _mm_hi(xi, cti)).astype(BF16)
        s = tc - 1 - k
        bexp_ref[s * LANES:(s + 1) * LANES, 0:ns] = xr.astype(BF16)
        bexp_ref[s * LANES:(s + 1) * LANES, ns:2 * ns] = xi.astype(BF16)
        er, ei = pw_col[k + 1]
        cexp_ref[0:ns, k * LANES:(k + 1) * LANES] = (ctr * er - cti * ei).astype(BF16)
        cexp_ref[ns:2 * ns, k * LANES:(k + 1) * LANES] = (-(ctr * ei + cti * er)).astype(BF16)
    a_ref[:, 0:ns] = pw_row[tc][0]
    a_ref[:, ns:2 * ns] = pw_row[tc][1]


def _s5_main_kernel(u_ref, bd_ref, bexp_ref, cexp_ref, a_ref, y_ref, m_ref, hl_ref, hp_ref):
    tc = S5_TC
    ns = a_ref.shape[-1] // 2
    rows = u_ref.shape[0]

    @pl.when(pl.program_id(1) == 0)
    def _():
        for s in range(tc):
            for t in range(tc):
                blk = bd_ref[t - s] if t >= s else jnp.zeros((LANES, LANES), BF16)
                m_ref[s * LANES:(s + 1) * LANES, t * LANES:(t + 1) * LANES] = blk

    u = u_ref[...].astype(BF16)
    hl_ref[...] = jnp.dot(u, bexp_ref[...], preferred_element_type=F32)
    are, aim = a_ref[:, 0:ns], a_ref[:, ns:2 * ns]

    def step(c, carry):
        hre, him = carry
        hp_ref[pl.ds(c, 1), 0:ns] = hre
        hp_ref[pl.ds(c, 1), ns:2 * ns] = him
        xre = hl_ref[pl.ds(c, 1), 0:ns]
        xim = hl_ref[pl.ds(c, 1), ns:2 * ns]
        return are * hre - aim * him + xre, are * him + aim * hre + xim

    z = jnp.zeros((1, ns), F32)
    lax.fori_loop(0, rows, step, (z, z))
    y_ref[...] = (jnp.dot(u, m_ref[...], preferred_element_type=F32)
                  + jnp.dot(hp_ref[...].astype(BF16), cexp_ref[...], preferred_element_type=F32))


def _s5_post_kernel(y_ref, u_ref, d_ref, wv_ref, wg_ref, x_ref, o_ref, a_ref):
    @pl.when(pl.program_id(1) == 0)
    def _():
        y = y_ref[...] + d_ref[...] * u_ref[...]
        c = math.sqrt(2.0 / math.pi)
        a_ref[...] = (0.5 * y * (1.0 + jnp.tanh(c * (y + 0.044715 * (y * y * y))))).astype(BF16)

    a = a_ref[...]
    val = jnp.dot(a, wv_ref[...].astype(BF16), preferred_element_type=F32)
    gate = jnp.dot(a, wg_ref[...].astype(BF16), preferred_element_type=F32)
    o_ref[...] = x_ref[...] + val * _sigmoid(gate)


def _s5_mixer(x, g, lam_re, lam_im, log_dt, b_re, b_im, c_re, c_im, d_skip, w_glu, bsz):
    T, D = x.shape
    L = T // bsz
    G, P = lam_re.shape
    tc, gp = S5_TC, S5_GP
    ng = G // gp
    ns = gp * P
    rows = L // tc
    u = _rms(x, g)

    eye = jnp.eye(gp, dtype=F32)

    def bt(b):
        return jnp.einsum('ngph,gk->nghkp', b.reshape(ng, gp, P, S5_GROUP), eye).reshape(ng, LANES, ns)

    def ct(c):
        return jnp.einsum('nghp,gk->ngpkh', c.reshape(ng, gp, S5_GROUP, P), eye).reshape(ng, ns, LANES)

    lr, li = lam_re.reshape(ng, 1, ns), lam_im.reshape(ng, 1, ns)
    ldt = jnp.broadcast_to(log_dt[:, None], (G, P)).reshape(ng, 1, ns)
    row_spec = pl.BlockSpec((None, 1, ns), lambda n: (n, 0, 0))
    col_spec = pl.BlockSpec((None, ns, 1), lambda n: (n, 0, 0))
    bd, bexp, cexp, a16 = pl.pallas_call(
        _s5_prep_kernel,
        out_shape=(jax.ShapeDtypeStruct((ng, tc, LANES, LANES), BF16),
                   jax.ShapeDtypeStruct((ng, tc * LANES, 2 * ns), BF16),
                   jax.ShapeDtypeStruct((ng, 2 * ns, tc * LANES), BF16),
                   jax.ShapeDtypeStruct((ng, 1, 2 * ns), F32)),
        grid=(ng,),
        in_specs=[row_spec, row_spec, row_spec, col_spec, col_spec, col_spec,
                  pl.BlockSpec((None, LANES, ns), lambda n: (n, 0, 0)),
                  pl.BlockSpec((None, LANES, ns), lambda n: (n, 0, 0)),
                  pl.BlockSpec((None, ns, LANES), lambda n: (n, 0, 0)),
                  pl.BlockSpec((None, ns, LANES), lambda n: (n, 0, 0))],
        out_specs=(pl.BlockSpec((None, tc, LANES, LANES), lambda n: (n, 0, 0, 0)),
                   pl.BlockSpec((None, tc * LANES, 2 * ns), lambda n: (n, 0, 0)),
                   pl.BlockSpec((None, 2 * ns, tc * LANES), lambda n: (n, 0, 0)),
                   pl.BlockSpec((None, 1, 2 * ns), lambda n: (n, 0, 0))),
        compiler_params=_cparams(("parallel",)),
        name="s5_prep",
    )(lr, li, ldt, lr.reshape(ng, ns, 1), li.reshape(ng, ns, 1), ldt.reshape(ng, ns, 1),
      bt(b_re), bt(b_im), ct(c_re), ct(c_im))

    ug = u.reshape(bsz, rows, tc, ng, LANES).transpose(3, 0, 1, 2, 4).reshape(ng, bsz, rows, tc * LANES)
    yg = pl.pallas_call(
        _s5_main_kernel,
        out_shape=jax.ShapeDtypeStruct((ng, bsz, rows, tc * LANES), F32),
        grid=(ng, bsz),
        in_specs=[pl.BlockSpec((None, None, rows, tc * LANES), lambda n, b: (n, b, 0, 0)),
                  pl.BlockSpec((None, tc, LANES, LANES), lambda n, b: (n, 0, 0, 0)),
                  pl.BlockSpec((None, tc * LANES, 2 * ns), lambda n, b: (n, 0, 0)),
                  pl.BlockSpec((None, 2 * ns, tc * LANES), lambda n, b: (n, 0, 0)),
                  pl.BlockSpec((None, 1, 2 * ns), lambda n, b: (n, 0, 0))],
        out_specs=pl.BlockSpec((None, None, rows, tc * LANES), lambda n, b: (n, b, 0, 0)),
        scratch_shapes=[pltpu.VMEM((tc * LANES, tc * LANES), BF16),
                        pltpu.VMEM((rows, 2 * ns), F32),
                        pltpu.VMEM((rows, 2 * ns), F32)],
        compiler_params=_cparams(("parallel", "arbitrary")),
        name="s5_main",
    )(ug, bd, bexp, cexp, a16)
    y = yg.reshape(ng, bsz, rows, tc, LANES).transpose(1, 2, 3, 0, 4).reshape(T, D)

    tm = min(1024, T)
    tn = 512
    nj = D // tn
    return pl.pallas_call(
        _s5_post_kernel,
        out_shape=jax.ShapeDtypeStruct((T, D), F32),
        grid=(T // tm, nj),
        in_specs=[pl.BlockSpec((tm, D), lambda i, j: (i, 0)),
                  pl.BlockSpec((tm, D), lambda i, j: (i, 0)),
                  pl.BlockSpec((1, D), lambda i, j: (0, 0)),
                  pl.BlockSpec((D, tn), lambda i, j: (0, j)),
                  pl.BlockSpec((D, tn), lambda i, j: (0, j + nj)),
                  pl.BlockSpec((tm, tn), lambda i, j: (i, j))],
        out_specs=pl.BlockSpec((tm, tn), lambda i, j: (i, j)),
        scratch_shapes=[pltpu.VMEM((tm, D), BF16)],
        compiler_params=_cparams(("parallel", "arbitrary")),
        name="s5_post",
    )(y, u, d_skip.reshape(1, D), w_glu, w_glu, x)


HG_SUPER = 256


def _hgrn_kernel(q_ref, f_ref, i_ref, g_ref, lb_ref, ng_ref, o_ref, st_ref, *, layer_idx):
    L, dk = q_ref.shape
    C, SC = HG_CHUNK, HG_SUPER
    lbraw = lb_ref[...]
    e = jnp.exp(lbraw - jnp.max(lbraw, axis=0, keepdims=True))
    sm = e / jnp.sum(e, axis=0, keepdims=True)
    layer = lax.broadcasted_iota(jnp.int32, sm.shape, 0)
    lb = jnp.sum(jnp.where((layer >= 1) & (layer <= layer_idx), sm, 0.0), axis=0, keepdims=True)

    ti = lax.broadcasted_iota(jnp.int32, (SC, SC), 0)
    si = lax.broadcasted_iota(jnp.int32, (SC, SC), 1)
    causal = ((ti // C) == (si // C)) & (si <= ti)
    pos = lax.broadcasted_iota(jnp.int32, (SC, dk), 0) & (C - 1)
    ng = ng_ref[...]
    st_ref[...] = jnp.zeros_like(st_ref)

    def body(sc, _):
        r0 = pl.multiple_of(sc * SC, SC)
        f = f_ref[pl.ds(r0, SC), :]
        fgate = lb + (1.0 - lb) * _sigmoid(f)
        logf = jnp.log(fgate)
        k = (1.0 - lb) * _sigmoid(-f)
        q = _silu(q_ref[pl.ds(r0, SC), :]) * dk ** -0.5
        v = i_ref[pl.ds(r0, SC), :]
        b, rest = logf, logf
        shift = 1
        while shift < C:
            b = b + jnp.where(pos >= shift, pltpu.roll(b, shift, 0), 0.0)
            rest = rest + jnp.where(pos + shift < C, pltpu.roll(rest, SC - shift, 0), 0.0)
            shift *= 2
        q_t = q * jnp.exp(b)
        k_t = k * jnp.exp(-b)
        k_state = k * jnp.exp(rest - logf)
        attn = jnp.where(causal, _mm_nt(q_t, k_t), 0.0)
        o = _mm(attn, v)
        nc = SC // C
        outer = [_mm_tn(v[c * C:(c + 1) * C], k_state[c * C:(c + 1) * C]) for c in range(nc)]
        states = [st_ref[...]]
        for c in range(nc):
            dc = jnp.exp(b[(c + 1) * C - 1:(c + 1) * C, :])
            states.append(states[c] * dc + outer[c])
        st_ref[...] = states[nc]
        inter = [_mm_nt(q_t[c * C:(c + 1) * C], states[c]) for c in range(nc)]
        o = o + jnp.concatenate(inter, axis=0)
        o = _rms_rows(o, ng) * _silu(g_ref[pl.ds(r0, SC), :])
        o_ref[pl.ds(r0, SC), :] = o
        return 0

    lax.fori_loop(0, L // SC, body, 0)


def _hgrn_mixer(x, g, w_in, lower_bounds, norm_g, w_out, bsz, layer_idx):
    T, D = x.shape
    L = T // bsz
    H = HG_HEADS
    dk = D // H
    proj = _rms_matmul(x, g, w_in, 512).reshape(bsz, L, 4 * D)
    nl = lower_bounds.shape[0]

    def col(off):
        return pl.BlockSpec((None, L, dk), lambda b, h: (b, 0, off * H + h))

    o = pl.pallas_call(
        functools.partial(_hgrn_kernel, layer_idx=layer_idx),
        out_shape=jax.ShapeDtypeStruct((bsz, L, D), F32),
        grid=(bsz, H),
        in_specs=[col(0), col(1), col(2), col(3),
                  pl.BlockSpec((nl, dk), lambda b, h: (0, h)),
                  pl.BlockSpec((1, dk), lambda b, h: (0, 0))],
        out_specs=pl.BlockSpec((None, L, dk), lambda b, h: (b, 0, h)),
        scratch_shapes=[pltpu.VMEM((dk, dk), F32)],
        compiler_params=_cparams(("parallel", "parallel")),
        name="hgrn",
    )(proj, proj, proj, proj, lower_bounds, norm_g.reshape(1, dk))
    return _matmul_residual(o.reshape(T, D), w_out, x)


GDN_LTILE = 512
GDN_QK_PER_STEP = 4
GDN_SOLVE_WIDE_STEPS = 2


def _split3(x):
    x1 = x.astype(BF16)
    r = x - x1.astype(F32)
    x2 = r.astype(BF16)
    return x1, x2, (r - x2.astype(F32)).astype(BF16)


def _mm_sel(x, sel):
    s = sel.astype(BF16)
    p1, p2, p3 = (jnp.dot(p, s, preferred_element_type=F32) for p in _split3(x))
    return p1 + (p2 + p3)


def _mm_sel_nt(sel, x):
    s = sel.astype(BF16)
    p1, p2, p3 = (lax.dot_general(s, p, (((1,), (1,)), ((), ())), preferred_element_type=F32)
                  for p in _split3(x))
    return p1 + (p2 + p3)


def _mm3(a, b):
    a1 = a.astype(BF16)
    a2 = (a - a1.astype(F32)).astype(BF16)
    b1 = b.astype(BF16)
    b2 = (b - b1.astype(F32)).astype(BF16)
    return (jnp.dot(a1, b1, preferred_element_type=F32)
            + (jnp.dot(a1, b2, preferred_element_type=F32) + jnp.dot(a2, b1, preferred_element_type=F32)))


def _gdn_kernel(q_ref, k_ref, v_ref, z_ref, gc_ref, cwq_ref, cwk_ref, cwv_ref, alog_ref, dtb_ref,
                ng_ref, o_ref, xq_ref, xk_ref, xv_ref, qn_ref, kn_ref, vc_ref, bcol_ref, gcol_ref,
                beta_ref, bcum_ref, u_ref, w_ref, qk_ref, s_ref):
    Lt = q_ref.shape[0]
    dk = LANES
    C = GDN_CHUNK
    nqk = GDN_QK_PER_STEP
    nh = nqk * (GDN_V_HEADS // GDN_QK_HEADS)
    hg = pl.program_id(1)
    first = pl.program_id(2) == 0
    pad = 8

    @pl.when(first)
    def _():
        s_ref[...] = jnp.zeros_like(s_ref)
        xq_ref[0:pad, :] = jnp.zeros((pad, xq_ref.shape[1]), F32)
        xk_ref[0:pad, :] = jnp.zeros((pad, xk_ref.shape[1]), F32)
        xv_ref[0:pad, :] = jnp.zeros((pad, xv_ref.shape[1]), F32)

    def conv_silu(src_ref, w_ref, xp_ref):
        xp_ref[pad:pad + Lt, :] = src_ref[...]
        w = w_ref[...]
        acc = None
        for j in range(GDN_CONV):
            s = pad - (GDN_CONV - 1) + j
            term = w[j:j + 1, :] * xp_ref[s:s + Lt, :]
            acc = term if acc is None else acc + term
        xp_ref[0:pad, :] = xp_ref[Lt:Lt + pad, :]
        return _silu(acc)

    def l2n(t):
        return t * lax.rsqrt(jnp.sum(t * t, axis=-1, keepdims=True) + 1e-6)

    qa = conv_silu(q_ref, cwq_ref, xq_ref)
    ka = conv_silu(k_ref, cwk_ref, xk_ref)
    for i in range(nqk):
        qn_ref[:, i * dk:(i + 1) * dk] = l2n(qa[:, i * dk:(i + 1) * dk]) * dk ** -0.5
        kn_ref[:, i * dk:(i + 1) * dk] = l2n(ka[:, i * dk:(i + 1) * dk])
    vc_ref[...] = conv_silu(v_ref, cwv_ref, xv_ref)

    gcraw = gc_ref[...]
    bcol_ref[...] = _sigmoid(gcraw)
    xs = gcraw + dtb_ref[...]
    softplus = jnp.maximum(xs, 0.0) + jnp.log(1.0 + jnp.exp(-jnp.abs(xs)))
    gcol_ref[...] = -jnp.exp(alog_ref[...]) * softplus

    row = lax.broadcasted_iota(jnp.int32, (Lt, LANES), 0) & (C - 1)
    bc_all = gcol_ref[...]
    shift = 1
    while shift < C:
        bc_all = bc_all + jnp.where(row >= shift, pltpu.roll(bc_all, shift, 0), 0.0)
        shift *= 2
    sel_r = lax.broadcasted_iota(jnp.int32, (LANES, LANES), 0)
    braw = bcol_ref[...]
    for j in range(nh):
        hv = nh * hg + j
        beta_ref[j] = _mm_sel(braw, jnp.where(sel_r == hv, 1.0, 0.0))
        bcum_ref[j] = _mm_sel(bc_all, jnp.where(sel_r == hv + GDN_V_HEADS, 1.0, 0.0))

    ti = lax.broadcasted_iota(jnp.int32, (C, C), 0)
    si = lax.broadcasted_iota(jnp.int32, (C, C), 1)
    incl = si <= ti
    strict = si < ti
    lane = lax.broadcasted_iota(jnp.int32, (C, LANES), 1)
    pick0 = jnp.where(lane == 0, 1.0, 0.0)
    ng = ng_ref[...]
    rep = nh // nqk

    def solve_body(c, _):
        r0 = pl.multiple_of(c * C, C)
        rows = pl.ds(r0, C)
        kk, qk0 = [], []
        for i in range(nqk):
            kc = kn_ref[rows, i * dk:(i + 1) * dk]
            kk.append(_mm_nt(kc, kc))
            qk0.append(_mm_nt(qn_ref[rows, i * dk:(i + 1) * dk], kc))
        heads = range(nh)
        brow = [_mm_sel_nt(pick0, bcum_ref[j, rows, :]) for j in heads]
        pmat, sol, decay = [], [], []
        for j in heads:
            i = j // rep
            bcum = bcum_ref[j, rows, :]
            beta = beta_ref[j, rows, :]
            dmat = bcum[:, 0:C] - brow[j]
            decay.append(jnp.where(incl, jnp.exp(jnp.where(incl, dmat, 0.0)), 0.0))
            pmat.append(jnp.where(strict, -(kk[i] * beta[:, 0:C]) * decay[j], 0.0))
            sol.append(jnp.concatenate([vc_ref[rows, j * dk:(j + 1) * dk] * beta,
                                        kn_ref[rows, i * dk:(i + 1) * dk] * beta * jnp.exp(bcum)],
                                       axis=1))
        for it in range(6):
            mm = _mm3 if it < GDN_SOLVE_WIDE_STEPS else _mm
            sol = [sol[j] + mm(pmat[j], sol[j]) for j in heads]
            if it < 5:
                pmat = [mm(pmat[j], pmat[j]) for j in heads]
        for j in heads:
            u_ref[j, rows, :] = sol[j][:, 0:dk]
            w_ref[j, rows, :] = sol[j][:, dk:2 * dk]
            qk_ref[j, rows, :] = qk0[j // rep] * decay[j]
        return 0

    lax.fori_loop(0, Lt // C, solve_body, 0)

    def state_body(c, _):
        r0 = pl.multiple_of(c * C, C)
        rows = pl.ds(r0, C)
        heads = range(nh)
        S, blast, k_state, ws = [], [], [], []
        for j in heads:
            i = j // rep
            bcum = bcum_ref[j, rows, :]
            blast.append(bcum_ref[j, pl.ds(r0 + C - 1, 1), :])
            q_dec = qn_ref[rows, i * dk:(i + 1) * dk] * jnp.exp(bcum)
            k_state.append(kn_ref[rows, i * dk:(i + 1) * dk] * jnp.exp(blast[j] - bcum))
            S.append(s_ref[j])
            ws.append(_mm(jnp.concatenate([w_ref[j, rows, :], q_dec], axis=0), S[j]))
        v_new = [u_ref[j, rows, :] - ws[j][0:C] for j in heads]
        for j in heads:
            s_ref[j] = S[j] * jnp.exp(blast[j]) + _mm_tn(k_state[j], v_new[j])
        for j in heads:
            o = ws[j][C:2 * C] + _mm(qk_ref[j, rows, :], v_new[j])
            z = z_ref[rows, j * dk:(j + 1) * dk]
            o_ref[rows, j * dk:(j + 1) * dk] = _rms_rows(o, ng) * _silu(z)
        return 0

    lax.fori_loop(0, Lt // C, state_body, 0)


def _gdn_mixer(x, g, w_in, conv_w, a_log, dt_bias, norm_g, w_out, bsz):
    T, D = x.shape
    L = T // bsz
    Hk, Hv = GDN_QK_HEADS, GDN_V_HEADS
    dk = LANES
    nqk = GDN_QK_PER_STEP
    nh = nqk * (Hv // Hk)
    ngrp = Hk // nqk
    qw, vw = nqk * dk, nh * dk
    lt = min(GDN_LTILE, L)
    n_in = w_in.shape[1]
    n_pad = -(-n_in // (7 * LANES)) * (7 * LANES)
    w_pad = jnp.pad(w_in, ((0, 0), (0, n_pad - n_in)))
    proj = _rms_matmul(x, g, w_pad, 7 * LANES).reshape(bsz, L, n_pad)
    qkv_w = 2 * Hk * dk + Hv * dk
    gate_blk = (qkv_w + Hv * dk) // LANES
    alog_row = jnp.zeros((1, LANES), F32).at[0, Hv:2 * Hv].set(a_log)
    dtb_row = jnp.zeros((1, LANES), F32).at[0, Hv:2 * Hv].set(dt_bias)
    o = pl.pallas_call(
        _gdn_kernel,
        out_shape=jax.ShapeDtypeStruct((bsz, L, Hv * dk), F32),
        grid=(bsz, ngrp, L // lt),
        in_specs=[pl.BlockSpec((None, lt, qw), lambda b, h, t: (b, t, h)),
                  pl.BlockSpec((None, lt, qw), lambda b, h, t: (b, t, ngrp + h)),
                  pl.BlockSpec((None, lt, vw), lambda b, h, t: (b, t, ngrp + h)),
                  pl.BlockSpec((None, lt, vw), lambda b, h, t: (b, t, 2 * ngrp + h)),
                  pl.BlockSpec((None, lt, LANES), lambda b, h, t: (b, t, gate_blk)),
                  pl.BlockSpec((GDN_CONV, qw), lambda b, h, t: (0, h)),
                  pl.BlockSpec((GDN_CONV, qw), lambda b, h, t: (0, ngrp + h)),
                  pl.BlockSpec((GDN_CONV, vw), lambda b, h, t: (0, ngrp + h)),
                  pl.BlockSpec((1, LANES), lambda b, h, t: (0, 0)),
                  pl.BlockSpec((1, LANES), lambda b, h, t: (0, 0)),
                  pl.BlockSpec((1, dk), lambda b, h, t: (0, 0))],
        out_specs=pl.BlockSpec((None, lt, vw), lambda b, h, t: (b, t, h)),
        scratch_shapes=[pltpu.VMEM((lt + 8, qw), F32),
                        pltpu.VMEM((lt + 8, qw), F32),
                        pltpu.VMEM((lt + 8, vw), F32),
                        pltpu.VMEM((lt, qw), F32),
                        pltpu.VMEM((lt, qw), F32),
                        pltpu.VMEM((lt, vw), F32),
                        pltpu.VMEM((lt, LANES), F32),
                        pltpu.VMEM((lt, LANES), F32),
                        pltpu.VMEM((nh, lt, LANES), F32),
                        pltpu.VMEM((nh, lt, LANES), F32),
                        pltpu.VMEM((nh, lt, dk), F32),
                        pltpu.VMEM((nh, lt, dk), F32),
                        pltpu.VMEM((nh, lt, GDN_CHUNK), F32),
                        pltpu.VMEM((nh, dk, dk), F32)],
        compiler_params=_cparams(("parallel", "parallel", "arbitrary")),
        name="gdn",
    )(proj, proj, proj, proj, proj, conv_w, conv_w, conv_w, alog_row, dtb_row, norm_g.reshape(1, dk))
    return _matmul_residual(o.reshape(T, Hv * dk), w_out, x)


def _t5_bucket_np(n):
    max_exact = REL_BUCKETS // 2
    nf = np.maximum(n, 1).astype(np.float32)
    large = max_exact + (np.log(nf / np.float32(max_exact)) / np.float32(math.log(REL_MAX_DIST / max_exact))
                         * np.float32(REL_BUCKETS - max_exact)).astype(np.int32)
    large = np.minimum(large, REL_BUCKETS - 1)
    return np.where(n < max_exact, n, large).astype(np.int32)


def _bias_vec_kernel(bucket_ref, table_ref, o_ref):
    bucket = bucket_ref[...]
    table = table_ref[...]
    acc = jnp.zeros(o_ref.shape, F32)
    for k in range(REL_BUCKETS):
        acc = acc + jnp.where(bucket == k, table[:, k:k + 1], 0.0)
    o_ref[...] = acc


def _moba_kernel(q_ref, k_ref, v_ref, bias_ref, o_ref, kmean_ref, tiles_ref, s_ref, mx_ref, ls_ref,
                 acc_ref, *, L):
    BS = MOBA_BLOCK
    dh = q_ref.shape[-1]
    nb = L // BS
    qb = pl.program_id(2)
    rev_len = bias_ref.shape[-1]

    @pl.when((pl.program_id(1) == 0) & (qb == 0))
    def _():
        for d in range(nb):
            start = rev_len - 2 * BS - d * BS
            wb = jnp.broadcast_to(bias_ref[:, start:start + 2 * BS], (BS, 2 * BS))
            tiles_ref[d] = pltpu.roll(wb, BS + 1, 1, stride=1, stride_axis=0)[:, 0:BS]

    @pl.when(qb == 0)
    def _():
        kmean_ref[...] = jnp.zeros_like(kmean_ref)
        for n in range(nb):
            kmean_ref[n:n + 1, :] = jnp.sum(k_ref[n * BS:(n + 1) * BS, :], axis=0, keepdims=True) / BS

    q = q_ref[...]
    lane = lax.broadcasted_iota(jnp.int32, (BS, LANES), 1)
    past = lane < qb
    gate = jnp.where(past, _mm_nt_hi(q, kmean_ref[...]), -1e30)
    sel = jnp.zeros((BS, LANES), F32)
    for _ in range(MOBA_TOPK):
        mx = jnp.max(gate, axis=1, keepdims=True)
        first = jnp.min(jnp.where(gate == mx, lane, LANES), axis=1, keepdims=True)
        pick = lane == first
        sel = jnp.where(pick & past, 1.0, sel)
        gate = jnp.where(pick, -jnp.inf, gate)

    scale = dh ** -0.5
    qbf = q.astype(BF16)
    selbf = sel.astype(BF16)
    half = BS // 2

    def fold(t):
        return jnp.maximum(t[:, 0:half], t[:, half:BS])

    qi = lax.broadcasted_iota(jnp.int32, (BS, BS), 0)
    ki = lax.broadcasted_iota(jnp.int32, (BS, BS), 1)
    own0 = pl.multiple_of(qb * BS, BS)
    s = _mm_nt(qbf, k_ref[pl.ds(own0, BS), :]) * scale + tiles_ref[0]
    s = jnp.where(ki <= qi, s, -jnp.inf)
    s_ref[qb] = s
    mx_ref[...] = fold(s)
    blk = lax.broadcasted_iota(jnp.int32, (LANES, BS), 0)

    def logits(j, _):
        r0 = pl.multiple_of(j * BS, BS)
        s = _mm_nt(qbf, k_ref[pl.ds(r0, BS), :]) * scale + tiles_ref[qb - j]
        chosen = jnp.dot(selbf, jnp.where(blk == j, 1.0, 0.0).astype(BF16),
                         preferred_element_type=F32)
        s = jnp.where(chosen > 0.5, s, -jnp.inf)
        s_ref[j] = s
        mx_ref[...] = jnp.maximum(mx_ref[...], fold(s))
        return 0

    lax.fori_loop(0, qb, logits, 0)
    m = jnp.max(mx_ref[...], axis=1, keepdims=True)

    ls_ref[...] = jnp.zeros_like(ls_ref)
    acc_ref[...] = jnp.zeros_like(acc_ref)

    def accumulate(j, _):
        r0 = pl.multiple_of(j * BS, BS)
        p = jnp.exp(s_ref[j] - m)
        ls_ref[...] += p[:, 0:half] + p[:, half:BS]
        acc_ref[...] += _mm(p, v_ref[pl.ds(r0, BS), :])
        return 0

    lax.fori_loop(0, qb + 1, accumulate, 0)
    o_ref[...] = acc_ref[...] / jnp.sum(ls_ref[...], axis=1, keepdims=True)


def _moba_mixer(x, g, w_in, rel_table, w_out, bsz):
    T, D = x.shape
    L = T // bsz
    H, dh, BS = MOBA_HEADS, LANES, MOBA_BLOCK
    nb = L // BS
    proj = _rms_matmul(x, g, w_in, 512).reshape(bsz, L, 3 * D)
    rev_len = L + BS
    dist = np.maximum(L - 1 - np.arange(rev_len), 0)
    bucket_rev = jnp.asarray(_t5_bucket_np(dist)[None, :])
    bias_rev = pl.pallas_call(
        _bias_vec_kernel,
        out_shape=jax.ShapeDtypeStruct((H, rev_len), F32),
        name="moba_bias",
    )(bucket_rev, rel_table.T).reshape(H, 1, rev_len)
    o = pl.pallas_call(
        functools.partial(_moba_kernel, L=L),
        out_shape=jax.ShapeDtypeStruct((bsz, L, D), F32),
        grid=(H, bsz, nb),
        in_specs=[pl.BlockSpec((None, BS, dh), lambda h, b, n: (b, n, h)),
                  pl.BlockSpec((None, L, dh), lambda h, b, n: (b, 0, H + h)),
                  pl.BlockSpec((None, L, dh), lambda h, b, n: (b, 0, 2 * H + h)),
                  pl.BlockSpec((None, 1, rev_len), lambda h, b, n: (h, 0, 0))],
        out_specs=pl.BlockSpec((None, BS, dh), lambda h, b, n: (b, n, h)),
        scratch_shapes=[pltpu.VMEM((LANES, dh), F32),
                        pltpu.VMEM((nb, BS, BS), F32),
                        pltpu.VMEM((nb, BS, BS), F32),
                        pltpu.VMEM((BS, BS // 2), F32),
                        pltpu.VMEM((BS, BS // 2), F32),
                        pltpu.VMEM((BS, dh), F32)],
        compiler_params=_cparams(("parallel", "arbitrary", "arbitrary")),
        name="moba",
    )(proj, proj, proj, bias_rev)
    return _matmul_residual(o.reshape(T, D), w_out, x)


def kernel(x, norm_g, final_norm_g, ffn_w_gate, ffn_w_up, ffn_w_down, s5_lam_re, s5_lam_im, s5_log_dt, s5_b_re, s5_b_im, s5_c_re, s5_c_im, s5_d, s5_w_glu, hg_w_in, hg_lower_bounds, hg_norm_g, hg_w_out, gdn_w_in, gdn_conv_w, gdn_a_log, gdn_dt_bias, gdn_norm_g, gdn_w_out, moba_w_in, moba_w_out, rel_bias_table):
    bsz, L, D = x.shape
    depth = norm_g.shape[0]
    h = x.reshape(bsz * L, D)
    for i in range(depth):
        m, j = i % 4, i // 4
        h = _ffn(h, norm_g[i, 0], ffn_w_gate[i, 0], ffn_w_up[i, 0], ffn_w_down[i, 0])
        if m == 0:
            h = _s5_mixer(h, norm_g[i, 1], s5_lam_re[j], s5_lam_im[j], s5_log_dt[j], s5_b_re[j],
                          s5_b_im[j], s5_c_re[j], s5_c_im[j], s5_d[j], s5_w_glu[j], bsz)
        elif m == 1:
            h = _hgrn_mixer(h, norm_g[i, 1], hg_w_in[j], hg_lower_bounds, hg_norm_g[j], hg_w_out[j],
                            bsz, i)
        elif m == 2:
            h = _gdn_mixer(h, norm_g[i, 1], gdn_w_in[j], gdn_conv_w[j], gdn_a_log[j], gdn_dt_bias[j],
                           gdn_norm_g[j], gdn_w_out[j], bsz)
        else:
            h = _moba_mixer(h, norm_g[i, 1], moba_w_in[j], rel_bias_table, moba_w_out[j], bsz)
        h = _ffn(h, norm_g[i, 2], ffn_w_gate[i, 1], ffn_w_up[i, 1], ffn_w_down[i, 1])
    return _rms(h, final_norm_g).reshape(bsz, L, D)
```

```python
import functools
import math

import jax
import jax.numpy as jnp
import numpy as np
from jax import lax
from jax.experimental import pallas as pl
from jax.experimental.pallas import tpu as pltpu

F32 = jnp.float32
BF16 = jnp.bfloat16
LANES = 128
RMS_EPS = 1e-6

HG_HEADS = 8
HG_CHUNK = 32
GDN_QK_HEADS = 8
GDN_V_HEADS = 16
GDN_CHUNK = 64
GDN_CONV = 4
MOBA_HEADS = 8
MOBA_BLOCK = 256
MOBA_TOPK = 3
REL_BUCKETS = 32
REL_MAX_DIST = 2048
S5_GROUP = 16
S5_STATE = 64
S5_TC = 16
S5_GP = LANES // S5_GROUP

VMEM_LIMIT = 56 * 1024 * 1024


def _cparams(sem):
    return pltpu.CompilerParams(dimension_semantics=sem, vmem_limit_bytes=VMEM_LIMIT)


def _mm(a, b):
    return jnp.dot(a.astype(BF16), b.astype(BF16), preferred_element_type=F32)


def _mm_nt(a, b):
    return lax.dot_general(a.astype(BF16), b.astype(BF16), (((1,), (1,)), ((), ())),
                           preferred_element_type=F32)


def _mm_tn(a, b):
    return lax.dot_general(a.astype(BF16), b.astype(BF16), (((0,), (0,)), ((), ())),
                           preferred_element_type=F32)


def _mm_hi(a, b):
    return jnp.dot(a, b, precision=lax.Precision.HIGHEST, preferred_element_type=F32)


def _mm_nt_hi(a, b):
    return lax.dot_general(a, b, (((1,), (1,)), ((), ())), precision=lax.Precision.HIGHEST,
                           preferred_element_type=F32)


def _sigmoid(x):
    return 1.0 / (1.0 + jnp.exp(-x))


def _silu(x):
    return x * _sigmoid(x)


def _rms_rows(x, g):
    return x * lax.rsqrt(jnp.mean(x * x, axis=-1, keepdims=True) + RMS_EPS) * g


def _ffn_kernel(x_ref, g_ref, wg_ref, wu_ref, wd_ref, o_ref, h_ref, acc_ref):
    j = pl.program_id(1)

    @pl.when(j == 0)
    def _():
        h_ref[...] = _rms_rows(x_ref[...], g_ref[...]).astype(BF16)
        acc_ref[...] = jnp.zeros_like(acc_ref)

    h = h_ref[...]
    gate = jnp.dot(h, wg_ref[...].astype(BF16), preferred_element_type=F32)
    up = jnp.dot(h, wu_ref[...].astype(BF16), preferred_element_type=F32)
    a = (_silu(gate) * up).astype(BF16)
    acc_ref[...] += jnp.dot(a, wd_ref[...].astype(BF16), preferred_element_type=F32)

    @pl.when(j == pl.num_programs(1) - 1)
    def _():
        o_ref[...] = x_ref[...] + 0.5 * acc_ref[...]


def _ffn(x, g, wg, wu, wd):
    T, D = x.shape
    F = wg.shape[1]
    tm = min(1024, T)
    tf = 256
    return pl.pallas_call(
        _ffn_kernel,
        out_shape=jax.ShapeDtypeStruct((T, D), F32),
        grid=(T // tm, F // tf),
        in_specs=[pl.BlockSpec((tm, D), lambda i, j: (i, 0)),
                  pl.BlockSpec((1, D), lambda i, j: (0, 0)),
                  pl.BlockSpec((D, tf), lambda i, j: (0, j)),
                  pl.BlockSpec((D, tf), lambda i, j: (0, j)),
                  pl.BlockSpec((tf, D), lambda i, j: (j, 0))],
        out_specs=pl.BlockSpec((tm, D), lambda i, j: (i, 0)),
        scratch_shapes=[pltpu.VMEM((tm, D), BF16), pltpu.VMEM((tm, D), F32)],
        compiler_params=_cparams(("parallel", "arbitrary")),
        name="ffn",
    )(x, g.reshape(1, D), wg, wu, wd)


def _rms_kernel(x_ref, g_ref, o_ref):
    o_ref[...] = _rms_rows(x_ref[...], g_ref[...])


def _rms(x, g):
    T, D = x.shape
    tm = min(1024, T)
    return pl.pallas_call(
        _rms_kernel,
        out_shape=jax.ShapeDtypeStruct((T, D), F32),
        grid=(T // tm,),
        in_specs=[pl.BlockSpec((tm, D), lambda i: (i, 0)), pl.BlockSpec((1, D), lambda i: (0, 0))],
        out_specs=pl.BlockSpec((tm, D), lambda i: (i, 0)),
        compiler_params=_cparams(("parallel",)),
        name="rms",
    )(x, g.reshape(1, D))


def _rmsmm_kernel(x_ref, g_ref, w_ref, o_ref, h_ref):
    @pl.when(pl.program_id(1) == 0)
    def _():
        h_ref[...] = _rms_rows(x_ref[...], g_ref[...]).astype(BF16)

    o_ref[...] = jnp.dot(h_ref[...], w_ref[...].astype(BF16), preferred_element_type=F32)


def _rms_matmul(x, g, w, tn):
    T, D = x.shape
    N = w.shape[1]
    tm = min(1024, T)
    return pl.pallas_call(
        _rmsmm_kernel,
        out_shape=jax.ShapeDtypeStruct((T, N), F32),
        grid=(T // tm, N // tn),
        in_specs=[pl.BlockSpec((tm, D), lambda i, j: (i, 0)),
                  pl.BlockSpec((1, D), lambda i, j: (0, 0)),
                  pl.BlockSpec((D, tn), lambda i, j: (0, j))],
        out_specs=pl.BlockSpec((tm, tn), lambda i, j: (i, j)),
        scratch_shapes=[pltpu.VMEM((tm, D), BF16)],
        compiler_params=_cparams(("parallel", "arbitrary")),
        name="rms_matmul",
    )(x, g.reshape(1, D), w)


def _mmres_kernel(a_ref, w_ref, x_ref, o_ref):
    o_ref[...] = x_ref[...] + _mm(a_ref[...], w_ref[...])


def _matmul_residual(a, w, x):
    T, K = a.shape
    N = w.shape[1]
    tm = min(512, T)
    return pl.pallas_call(
        _mmres_kernel,
        out_shape=jax.ShapeDtypeStruct((T, N), F32),
        grid=(T // tm,),
        in_specs=[pl.BlockSpec((tm, K), lambda i: (i, 0)),
                  pl.BlockSpec((K, N), lambda i: (0, 0)),
                  pl.BlockSpec((tm, N), lambda i: (i, 0))],
        out_specs=pl.BlockSpec((tm, N), lambda i: (i, 0)),
        compiler_params=_cparams(("parallel",)),
        name="matmul_residual",
    )(a, w, x)


def _cmul(ar, ai, br, bi):
    return ar * br - ai * bi, ar * bi + ai * br


def _s5_powers(lr, li, ldt, n):
    lr = jnp.minimum(lr, -1e-4)
    dt = jnp.exp(ldt)
    mag = jnp.exp(lr * dt)
    br = mag * jnp.cos(li * dt)
    bi = mag * jnp.sin(li * dt)
    den = lr * lr + li * li
    fr = ((br - 1.0) * lr + bi * li) / den
    fi = (bi * lr - (br - 1.0) * li) / den
    pw = [(jnp.ones_like(br), jnp.zeros_like(bi))]
    for _ in range(n):
        pw.append(_cmul(pw[-1][0], pw[-1][1], br, bi))
    return pw, (fr, fi)


def _s5_prep_kernel(lr_row, li_row, ldt_row, lr_col, li_col, ldt_col, btr_ref, bti_ref, ctr_ref,
                    cti_ref, bd_ref, bexp_ref, cexp_ref, a_ref):
    tc = S5_TC
    ns = lr_row.shape[-1]
    pw_row, (fr, fi) = _s5_powers(lr_row[...], li_row[...], ldt_row[...], tc)
    pw_col, _ = _s5_powers(lr_col[...], li_col[...], ldt_col[...], tc)
    btr, bti = btr_ref[...], bti_ref[...]
    ctr, cti = ctr_ref[...], cti_ref[...]
    for k in range(tc):
        fer, fei = _cmul(fr, fi, pw_row[k][0], pw_row[k][1])
        xr = btr * fer - bti * fei
        xi = btr * fei + bti * fer
        bd_ref[k] = (_mm_hi(xr, ctr) - _mm_hi(xi, cti)).astype(BF16)
        s = tc - 1 - k
        bexp_ref[s * LANES:(s + 1) * LANES, 0:ns] = xr.astype(BF16)
        bexp_ref[s * LANES:(s + 1) * LANES, ns:2 * ns] = xi.astype(BF16)
        er, ei = pw_col[k + 1]
        cexp_ref[0:ns, k * LANES:(k + 1) * LANES] = (ctr * er - cti * ei).astype(BF16)
        cexp_ref[ns:2 * ns, k * LANES:(k + 1) * LANES] = (-(ctr * ei + cti * er)).astype(BF16)
    a_ref[:, 0:ns] = pw_row[tc][0]
    a_ref[:, ns:2 * ns] = pw_row[tc][1]


def _s5_main_kernel(u_ref, bd_ref, bexp_ref, cexp_ref, a_ref, y_ref, m_ref, hl_ref, hp_ref):
    tc = S5_TC
    ns = a_ref.shape[-1] // 2
    rows = u_ref.shape[0]

    @pl.when(pl.program_id(1) == 0)
    def _():
        for s in range(tc):
            for t in range(tc):
                blk = bd_ref[t - s] if t >= s else jnp.zeros((LANES, LANES), BF16)
                m_ref[s * LANES:(s + 1) * LANES, t * LANES:(t + 1) * LANES] = blk

    u = u_ref[...].astype(BF16)
    hl_ref[...] = jnp.dot(u, bexp_ref[...], preferred_element_type=F32)
    are, aim = a_ref[:, 0:ns], a_ref[:, ns:2 * ns]

    def step(c, carry):
        hre, him = carry
        hp_ref[pl.ds(c, 1), 0:ns] = hre
        hp_ref[pl.ds(c, 1), ns:2 * ns] = him
        xre = hl_ref[pl.ds(c, 1), 0:ns]
        xim = hl_ref[pl.ds(c, 1), ns:2 * ns]
        return are * hre - aim * him + xre, are * him + aim * hre + xim

    z = jnp.zeros((1, ns), F32)
    lax.fori_loop(0, rows, step, (z, z))
    y_ref[...] = (jnp.dot(u, m_ref[...], preferred_element_type=F32)
                  + jnp.dot(hp_ref[...].astype(BF16), cexp_ref[...], preferred_element_type=F32))


def _s5_post_kernel(y_ref, u_ref, d_ref, wv_ref, wg_ref, x_ref, o_ref, a_ref):
    @pl.when(pl.program_id(1) == 0)
    def _():
        y = y_ref[...] + d_ref[...] * u_ref[...]
        c = math.sqrt(2.0 / math.pi)
        a_ref[...] = (0.5 * y * (1.0 + jnp.tanh(c * (y + 0.044715 * (y * y * y))))).astype(BF16)

    a = a_ref[...]
    val = jnp.dot(a, wv_ref[...].astype(BF16), preferred_element_type=F32)
    gate = jnp.dot(a, wg_ref[...].astype(BF16), preferred_element_type=F32)
    o_ref[...] = x_ref[...] + val * _sigmoid(gate)


def _s5_mixer(x, g, lam_re, lam_im, log_dt, b_re, b_im, c_re, c_im, d_skip, w_glu, bsz):
    T, D = x.shape
    L = T // bsz
    G, P = lam_re.shape
    tc, gp = S5_TC, S5_GP
    ng = G // gp
    ns = gp * P
    rows = L // tc
    u = _rms(x, g)

    eye = jnp.eye(gp, dtype=F32)

    def bt(b):
        return jnp.einsum('ngph,gk->nghkp', b.reshape(ng, gp, P, S5_GROUP), eye).reshape(ng, LANES, ns)

    def ct(c):
        return jnp.einsum('nghp,gk->ngpkh', c.reshape(ng, gp, S5_GROUP, P), eye).reshape(ng, ns, LANES)

    lr, li = lam_re.reshape(ng, 1, ns), lam_im.reshape(ng, 1, ns)
    ldt = jnp.broadcast_to(log_dt[:, None], (G, P)).reshape(ng, 1, ns)
    row_spec = pl.BlockSpec((None, 1, ns), lambda n: (n, 0, 0))
    col_spec = pl.BlockSpec((None, ns, 1), lambda n: (n, 0, 0))
    bd, bexp, cexp, a16 = pl.pallas_call(
        _s5_prep_kernel,
        out_shape=(jax.ShapeDtypeStruct((ng, tc, LANES, LANES), BF16),
                   jax.ShapeDtypeStruct((ng, tc * LANES, 2 * ns), BF16),
                   jax.ShapeDtypeStruct((ng, 2 * ns, tc * LANES), BF16),
                   jax.ShapeDtypeStruct((ng, 1, 2 * ns), F32)),
        grid=(ng,),
        in_specs=[row_spec, row_spec, row_spec, col_spec, col_spec, col_spec,
                  pl.BlockSpec((None, LANES, ns), lambda n: (n, 0, 0)),
                  pl.BlockSpec((None, LANES, ns), lambda n: (n, 0, 0)),
                  pl.BlockSpec((None, ns, LANES), lambda n: (n, 0, 0)),
                  pl.BlockSpec((None, ns, LANES), lambda n: (n, 0, 0))],
        out_specs=(pl.BlockSpec((None, tc, LANES, LANES), lambda n: (n, 0, 0, 0)),
                   pl.BlockSpec((None, tc * LANES, 2 * ns), lambda n: (n, 0, 0)),
                   pl.BlockSpec((None, 2 * ns, tc * LANES), lambda n: (n, 0, 0)),
                   pl.BlockSpec((None, 1, 2 * ns), lambda n: (n, 0, 0))),
        compiler_params=_cparams(("parallel",)),
        name="s5_prep",
    )(lr, li, ldt, lr.reshape(ng, ns, 1), li.reshape(ng, ns, 1), ldt.reshape(ng, ns, 1),
      bt(b_re), bt(b_im), ct(c_re), ct(c_im))

    ug = u.reshape(bsz, rows, tc, ng, LANES).transpose(3, 0, 1, 2, 4).reshape(ng, bsz, rows, tc * LANES)
    yg = pl.pallas_call(
        _s5_main_kernel,
        out_shape=jax.ShapeDtypeStruct((ng, bsz, rows, tc * LANES), F32),
        grid=(ng, bsz),
        in_specs=[pl.BlockSpec((None, None, rows, tc * LANES), lambda n, b: (n, b, 0, 0)),
                  pl.BlockSpec((None, tc, LANES, LANES), lambda n, b: (n, 0, 0, 0)),
                  pl.BlockSpec((None, tc * LANES, 2 * ns), lambda n, b: (n, 0, 0)),
                  pl.BlockSpec((None, 2 * ns, tc * LANES), lambda n, b: (n, 0, 0)),
                  pl.BlockSpec((None, 1, 2 * ns), lambda n, b: (n, 0, 0))],
        out_specs=pl.BlockSpec((None, None, rows, tc * LANES), lambda n, b: (n, b, 0, 0)),
        scratch_shapes=[pltpu.VMEM((tc * LANES, tc * LANES), BF16),
                        pltpu.VMEM((rows, 2 * ns), F32),
                        pltpu.VMEM((rows, 2 * ns), F32)],
        compiler_params=_cparams(("parallel", "arbitrary")),
        name="s5_main",
    )(ug, bd, bexp, cexp, a16)
    y = yg.reshape(ng, bsz, rows, tc, LANES).transpose(1, 2, 3, 0, 4).reshape(T, D)

    tm = min(1024, T)
    tn = 512
    nj = D // tn
    return pl.pallas_call(
        _s5_post_kernel,
        out_shape=jax.ShapeDtypeStruct((T, D), F32),
        grid=(T // tm, nj),
        in_specs=[pl.BlockSpec((tm, D), lambda i, j: (i, 0)),
                  pl.BlockSpec((tm, D), lambda i, j: (i, 0)),
                  pl.BlockSpec((1, D), lambda i, j: (0, 0)),
                  pl.BlockSpec((D, tn), lambda i, j: (0, j)),
                  pl.BlockSpec((D, tn), lambda i, j: (0, j + nj)),
                  pl.BlockSpec((tm, tn), lambda i, j: (i, j))],
        out_specs=pl.BlockSpec((tm, tn), lambda i, j: (i, j)),
        scratch_shapes=[pltpu.VMEM((tm, D), BF16)],
        compiler_params=_cparams(("parallel", "arbitrary")),
        name="s5_post",
    )(y, u, d_skip.reshape(1, D), w_glu, w_glu, x)


HG_SUPER = 256


def _hgrn_kernel(q_ref, f_ref, i_ref, g_ref, lb_ref, ng_ref, o_ref, st_ref, *, layer_idx):
    L, dk = q_ref.shape
    C, SC = HG_CHUNK, HG_SUPER
    lbraw = lb_ref[...]
    e = jnp.exp(lbraw - jnp.max(lbraw, axis=0, keepdims=True))
    sm = e / jnp.sum(e, axis=0, keepdims=True)
    layer = lax.broadcasted_iota(jnp.int32, sm.shape, 0)
    lb = jnp.sum(jnp.where((layer >= 1) & (layer <= layer_idx), sm, 0.0), axis=0, keepdims=True)

    ti = lax.broadcasted_iota(jnp.int32, (SC, SC), 0)
    si = lax.broadcasted_iota(jnp.int32, (SC, SC), 1)
    causal = ((ti // C) == (si // C)) & (si <= ti)
    pos = lax.broadcasted_iota(jnp.int32, (SC, dk), 0) & (C - 1)
    ng = ng_ref[...]
    st_ref[...] = jnp.zeros_like(st_ref)

    def body(sc, _):
        r0 = pl.multiple_of(sc * SC, SC)
        f = f_ref[pl.ds(r0, SC), :]
        fgate = lb + (1.0 - lb) * _sigmoid(f)
        logf = jnp.log(fgate)
        k = (1.0 - lb) * _sigmoid(-f)
        q = _silu(q_ref[pl.ds(r0, SC), :]) * dk ** -0.5
        v = i_ref[pl.ds(r0, SC), :]
        b, rest = logf, logf
        shift = 1
        while shift < C:
            b = b + jnp.where(pos >= shift, pltpu.roll(b, shift, 0), 0.0)
            rest = rest + jnp.where(pos + shift < C, pltpu.roll(rest, SC - shift, 0), 0.0)
            shift *= 2
        q_t = q * jnp.exp(b)
        k_t = k * jnp.exp(-b)
        k_state = k * jnp.exp(rest - logf)
        attn = jnp.where(causal, _mm_nt(q_t, k_t), 0.0)
        o = _mm(attn, v)
        nc = SC // C
        outer = [_mm_tn(v[c * C:(c + 1) * C], k_state[c * C:(c + 1) * C]) for c in range(nc)]
        states = [st_ref[...]]
        for c in range(nc):
            dc = jnp.exp(b[(c + 1) * C - 1:(c + 1) * C, :])
            states.append(states[c] * dc + outer[c])
        st_ref[...] = states[nc]
        inter = [_mm_nt(q_t[c * C:(c + 1) * C], states[c]) for c in range(nc)]
        o = o + jnp.concatenate(inter, axis=0)
        o = _rms_rows(o, ng) * _silu(g_ref[pl.ds(r0, SC), :])
        o_ref[pl.ds(r0, SC), :] = o
        return 0

    lax.fori_loop(0, L // SC, body, 0)


def _hgrn_mixer(x, g, w_in, lower_bounds, norm_g, w_out, bsz, layer_idx):
    T, D = x.shape
    L = T // bsz
    H = HG_HEADS
    dk = D // H
    proj = _rms_matmul(x, g, w_in, 512).reshape(bsz, L, 4 * D)
    nl = lower_bounds.shape[0]

    def col(off):
        return pl.BlockSpec((None, L, dk), lambda b, h: (b, 0, off * H + h))

    o = pl.pallas_call(
        functools.partial(_hgrn_kernel, layer_idx=layer_idx),
        out_shape=jax.ShapeDtypeStruct((bsz, L, D), F32),
        grid=(bsz, H),
        in_specs=[col(0), col(1), col(2), col(3),
                  pl.BlockSpec((nl, dk), lambda b, h: (0, h)),
                  pl.BlockSpec((1, dk), lambda b, h: (0, 0))],
        out_specs=pl.BlockSpec((None, L, dk), lambda b, h: (b, 0, h)),
        scratch_shapes=[pltpu.VMEM((dk, dk), F32)],
        compiler_params=_cparams(("parallel", "parallel")),
        name="hgrn",
    )(proj, proj, proj, proj, lower_bounds, norm_g.reshape(1, dk))
    return _matmul_residual(o.reshape(T, D), w_out, x)


GDN_LTILE = 512
GDN_QK_PER_STEP = 4
GDN_SOLVE_CHUNKS = 2
GDN_SOLVE_WIDE_STEPS = 1


def _split3(x):
    x1 = x.astype(BF16)
    r = x - x1.astype(F32)
    x2 = r.astype(BF16)
    return x1, x2, (r - x2.astype(F32)).astype(BF16)


def _mm_sel(x, sel):
    s = sel.astype(BF16)
    p1, p2, p3 = (jnp.dot(p, s, preferred_element_type=F32) for p in _split3(x))
    return p1 + (p2 + p3)


def _mm_sel_nt(sel, x):
    s = sel.astype(BF16)
    p1, p2, p3 = (lax.dot_general(s, p, (((1,), (1,)), ((), ())), preferred_element_type=F32)
                  for p in _split3(x))
    return p1 + (p2 + p3)


def _mm3(a, b):
    a1 = a.astype(BF16)
    a2 = (a - a1.astype(F32)).astype(BF16)
    b1 = b.astype(BF16)
    b2 = (b - b1.astype(F32)).astype(BF16)
    return (jnp.dot(a1, b1, preferred_element_type=F32)
            + (jnp.dot(a1, b2, preferred_element_type=F32) + jnp.dot(a2, b1, preferred_element_type=F32)))


def _gdn_kernel(q_ref, k_ref, v_ref, z_ref, gc_ref, cwq_ref, cwk_ref, cwv_ref, alog_ref, dtb_ref,
                ng_ref, o_ref, xq_ref, xk_ref, xv_ref, qn_ref, kn_ref, vc_ref, bcol_ref, gcol_ref,
                beta_ref, bcum_ref, u_ref, w_ref, qk_ref, s_ref):
    Lt = q_ref.shape[0]
    dk = LANES
    C = GDN_CHUNK
    nqk = GDN_QK_PER_STEP
    nh = nqk * (GDN_V_HEADS // GDN_QK_HEADS)
    hg = pl.program_id(1)
    first = pl.program_id(2) == 0
    pad = 8

    @pl.when(first)
    def _():
        s_ref[...] = jnp.zeros_like(s_ref)
        xq_ref[0:pad, :] = jnp.zeros((pad, xq_ref.shape[1]), F32)
        xk_ref[0:pad, :] = jnp.zeros((pad, xk_ref.shape[1]), F32)
        xv_ref[0:pad, :] = jnp.zeros((pad, xv_ref.shape[1]), F32)

    def conv_silu(src_ref, w_ref, xp_ref):
        xp_ref[pad:pad + Lt, :] = src_ref[...]
        w = w_ref[...]
        acc = None
        for j in range(GDN_CONV):
            s = pad - (GDN_CONV - 1) + j
            term = w[j:j + 1, :] * xp_ref[s:s + Lt, :]
            acc = term if acc is None else acc + term
        xp_ref[0:pad, :] = xp_ref[Lt:Lt + pad, :]
        return _silu(acc)

    def l2n(t):
        return t * lax.rsqrt(jnp.sum(t * t, axis=-1, keepdims=True) + 1e-6)

    qa = conv_silu(q_ref, cwq_ref, xq_ref)
    ka = conv_silu(k_ref, cwk_ref, xk_ref)
    for i in range(nqk):
        qn_ref[:, i * dk:(i + 1) * dk] = l2n(qa[:, i * dk:(i + 1) * dk]) * dk ** -0.5
        kn_ref[:, i * dk:(i + 1) * dk] = l2n(ka[:, i * dk:(i + 1) * dk])
    vc_ref[...] = conv_silu(v_ref, cwv_ref, xv_ref)

    gcraw = gc_ref[...]
    bcol_ref[...] = _sigmoid(gcraw)
    xs = gcraw + dtb_ref[...]
    softplus = jnp.maximum(xs, 0.0) + jnp.log(1.0 + jnp.exp(-jnp.abs(xs)))
    gcol_ref[...] = -jnp.exp(alog_ref[...]) * softplus

    row = lax.broadcasted_iota(jnp.int32, (Lt, LANES), 0) & (C - 1)
    bc_all = gcol_ref[...]
    shift = 1
    while shift < C:
        bc_all = bc_all + jnp.where(row >= shift, pltpu.roll(bc_all, shift, 0), 0.0)
        shift *= 2
    sel_r = lax.broadcasted_iota(jnp.int32, (LANES, LANES), 0)
    braw = bcol_ref[...]
    for j in range(nh):
        hv = nh * hg + j
        beta_ref[j] = _mm_sel(braw, jnp.where(sel_r == hv, 1.0, 0.0))
        bcum_ref[j] = _mm_sel(bc_all, jnp.where(sel_r == hv + GDN_V_HEADS, 1.0, 0.0))

    ti = lax.broadcasted_iota(jnp.int32, (C, C), 0)
    si = lax.broadcasted_iota(jnp.int32, (C, C), 1)
    incl = si <= ti
    strict = si < ti
    eye_c = jnp.where(si == ti, 1.0, 0.0)
    ng = ng_ref[...]
    rep = nh // nqk

    def solve_body(cc, _):
        rows = [pl.ds(pl.multiple_of((cc * GDN_SOLVE_CHUNKS + a) * C, C), C)
                for a in range(GDN_SOLVE_CHUNKS)]
        kk, qk0 = {}, {}
        for a, rw in enumerate(rows):
            for i in range(nqk):
                kc = kn_ref[rw, i * dk:(i + 1) * dk]
                kk[a, i] = _mm_nt(kc, kc)
                qk0[a, i] = _mm_nt(qn_ref[rw, i * dk:(i + 1) * dk], kc)
        probs = [(a, j) for a in range(GDN_SOLVE_CHUNKS) for j in range(nh)]
        pmat, sol, decay = {}, {}, {}
        for a, j in probs:
            i = j // rep
            bcum = bcum_ref[j, rows[a], :]
            beta = beta_ref[j, rows[a], :]
            dmat = bcum[:, 0:C] - jnp.transpose(bcum)[0:1, :]
            decay[a, j] = jnp.where(incl, jnp.exp(jnp.where(incl, dmat, 0.0)), 0.0)
            pmat[a, j] = jnp.where(strict, -(kk[a, i] * beta[:, 0:C]) * decay[a, j], 0.0)
            sol[a, j] = jnp.concatenate(
                [vc_ref[rows[a], j * dk:(j + 1) * dk] * beta,
                 kn_ref[rows[a], i * dk:(i + 1) * dk] * beta * jnp.exp(bcum)], axis=1)
        inv = {p: eye_c + pmat[p] for p in probs}
        pmat = {p: _mm3(pmat[p], pmat[p]) for p in probs}
        for it in range(1, 6):
            mm = _mm3 if it < GDN_SOLVE_WIDE_STEPS else _mm
            if it < 5:
                prod = {p: mm(pmat[p], jnp.concatenate([inv[p], pmat[p]], axis=1)) for p in probs}
                inv = {p: inv[p] + prod[p][:, 0:C] for p in probs}
                pmat = {p: prod[p][:, C:2 * C] for p in probs}
            else:
                inv = {p: inv[p] + mm(pmat[p], inv[p]) for p in probs}
        sol = {p: _mm3(inv[p], sol[p]) for p in probs}
        for a, j in probs:
            u_ref[j, rows[a], :] = sol[a, j][:, 0:dk]
            w_ref[j, rows[a], :] = sol[a, j][:, dk:2 * dk]
            qk_ref[j, rows[a], :] = qk0[a, j // rep] * decay[a, j]
        return 0

    lax.fori_loop(0, Lt // (C * GDN_SOLVE_CHUNKS), solve_body, 0)

    def state_body(c, _):
        r0 = pl.multiple_of(c * C, C)
        rows = pl.ds(r0, C)
        heads = range(nh)
        S, blast, k_state, ws = [], [], [], []
        for j in heads:
            i = j // rep
            bcum = bcum_ref[j, rows, :]
            blast.append(bcum_ref[j, pl.ds(r0 + C - 1, 1), :])
            q_dec = qn_ref[rows, i * dk:(i + 1) * dk] * jnp.exp(bcum)
            k_state.append(kn_ref[rows, i * dk:(i + 1) * dk] * jnp.exp(blast[j] - bcum))
            S.append(s_ref[j])
            ws.append(_mm(jnp.concatenate([w_ref[j, rows, :], q_dec], axis=0), S[j]))
        v_new = [u_ref[j, rows, :] - ws[j][0:C] for j in heads]
        for j in heads:
            s_ref[j] = S[j] * jnp.exp(blast[j]) + _mm_tn(k_state[j], v_new[j])
        for j in heads:
            o = ws[j][C:2 * C] + _mm(qk_ref[j, rows, :], v_new[j])
            z = z_ref[rows, j * dk:(j + 1) * dk]
            o_ref[rows, j * dk:(j + 1) * dk] = _rms_rows(o, ng) * _silu(z)
        return 0

    lax.fori_loop(0, Lt // C, state_body, 0)


def _gdn_mixer(x, g, w_in, conv_w, a_log, dt_bias, norm_g, w_out, bsz):
    T, D = x.shape
    L = T // bsz
    Hk, Hv = GDN_QK_HEADS, GDN_V_HEADS
    dk = LANES
    nqk = GDN_QK_PER_STEP
    nh = nqk * (Hv // Hk)
    ngrp = Hk // nqk
    qw, vw = nqk * dk, nh * dk
    lt = min(GDN_LTILE, L)
    n_in = w_in.shape[1]
    n_pad = -(-n_in // (7 * LANES)) * (7 * LANES)
    w_pad = jnp.pad(w_in, ((0, 0), (0, n_pad - n_in)))
    proj = _rms_matmul(x, g, w_pad, 7 * LANES).reshape(bsz, L, n_pad)
    qkv_w = 2 * Hk * dk + Hv * dk
    gate_blk = (qkv_w + Hv * dk) // LANES
    alog_row = jnp.zeros((1, LANES), F32).at[0, Hv:2 * Hv].set(a_log)
    dtb_row = jnp.zeros((1, LANES), F32).at[0, Hv:2 * Hv].set(dt_bias)
    o = pl.pallas_call(
        _gdn_kernel,
        out_shape=jax.ShapeDtypeStruct((bsz, L, Hv * dk), F32),
        grid=(bsz, ngrp, L // lt),
        in_specs=[pl.BlockSpec((None, lt, qw), lambda b, h, t: (b, t, h)),
                  pl.BlockSpec((None, lt, qw), lambda b, h, t: (b, t, ngrp + h)),
                  pl.BlockSpec((None, lt, vw), lambda b, h, t: (b, t, ngrp + h)),
                  pl.BlockSpec((None, lt, vw), lambda b, h, t: (b, t, 2 * ngrp + h)),
                  pl.BlockSpec((None, lt, LANES), lambda b, h, t: (b, t, gate_blk)),
                  pl.BlockSpec((GDN_CONV, qw), lambda b, h, t: (0, h)),
                  pl.BlockSpec((GDN_CONV, qw), lambda b, h, t: (0, ngrp + h)),
                  pl.BlockSpec((GDN_CONV, vw), lambda b, h, t: (0, ngrp + h)),
                  pl.BlockSpec((1, LANES), lambda b, h, t: (0, 0)),
                  pl.BlockSpec((1, LANES), lambda b, h, t: (0, 0)),
                  pl.BlockSpec((1, dk), lambda b, h, t: (0, 0))],
        out_specs=pl.BlockSpec((None, lt, vw), lambda b, h, t: (b, t, h)),
        scratch_shapes=[pltpu.VMEM((lt + 8, qw), F32),
                        pltpu.VMEM((lt + 8, qw), F32),
                        pltpu.VMEM((lt + 8, vw), F32),
                        pltpu.VMEM((lt, qw), F32),
                        pltpu.VMEM((lt, qw), F32),
                        pltpu.VMEM((lt, vw), F32),
                        pltpu.VMEM((lt, LANES), F32),
                        pltpu.VMEM((lt, LANES), F32),
                        pltpu.VMEM((nh, lt, LANES), F32),
                        pltpu.VMEM((nh, lt, LANES), F32),
                        pltpu.VMEM((nh, lt, dk), F32),
                        pltpu.VMEM((nh, lt, dk), F32),
                        pltpu.VMEM((nh, lt, GDN_CHUNK), F32),
                        pltpu.VMEM((nh, dk, dk), F32)],
        compiler_params=_cparams(("parallel", "parallel", "arbitrary")),
        name="gdn",
    )(proj, proj, proj, proj, proj, conv_w, conv_w, conv_w, alog_row, dtb_row, norm_g.reshape(1, dk))
    return _matmul_residual(o.reshape(T, Hv * dk), w_out, x)


def _t5_bucket_np(n):
    max_exact = REL_BUCKETS // 2
    nf = np.maximum(n, 1).astype(np.float32)
    large = max_exact + (np.log(nf / np.float32(max_exact)) / np.float32(math.log(REL_MAX_DIST / max_exact))
                         * np.float32(REL_BUCKETS - max_exact)).astype(np.int32)
    large = np.minimum(large, REL_BUCKETS - 1)
    return np.where(n < max_exact, n, large).astype(np.int32)


def _bias_vec_kernel(bucket_ref, table_ref, o_ref):
    bucket = bucket_ref[...]
    table = table_ref[...]
    acc = jnp.zeros(o_ref.shape, F32)
    for k in range(REL_BUCKETS):
        acc = acc + jnp.where(bucket == k, table[:, k:k + 1], 0.0)
    o_ref[...] = acc


def _moba_kernel(q_ref, k_ref, v_ref, bias_ref, o_ref, kmean_ref, tiles_ref, s_ref, mx_ref, ls_ref,
                 acc_ref, *, L):
    BS = MOBA_BLOCK
    dh = q_ref.shape[-1]
    nb = L // BS
    qb = pl.program_id(2)
    rev_len = bias_ref.shape[-1]

    @pl.when((pl.program_id(1) == 0) & (qb == 0))
    def _():
        for d in range(nb):
            start = rev_len - 2 * BS - d * BS
            wb = jnp.broadcast_to(bias_ref[:, start:start + 2 * BS], (BS, 2 * BS))
            tiles_ref[d] = pltpu.roll(wb, BS + 1, 1, stride=1, stride_axis=0)[:, 0:BS]

    @pl.when(qb == 0)
    def _():
        kmean_ref[...] = jnp.zeros_like(kmean_ref)
        for n in range(nb):
            kmean_ref[n:n + 1, :] = jnp.sum(k_ref[n * BS:(n + 1) * BS, :], axis=0, keepdims=True) / BS

    q = q_ref[...]
    blk_row = lax.broadcasted_iota(jnp.int32, (nb, BS), 0)
    past = blk_row < qb
    gate = jnp.where(past, _mm_nt_hi(kmean_ref[0:nb, :], q), -1e30)
    sel_t = jnp.zeros((nb, BS), F32)
    for _ in range(MOBA_TOPK):
        mx = jnp.max(gate, axis=0, keepdims=True)
        first = jnp.min(jnp.where(gate == mx, blk_row, nb), axis=0, keepdims=True)
        pick = blk_row == first
        sel_t = jnp.where(pick & past, 1.0, sel_t)
        gate = jnp.where(pick, -jnp.inf, gate)
    eye = jnp.where(lax.broadcasted_iota(jnp.int32, (nb, LANES), 0)
                    == lax.broadcasted_iota(jnp.int32, (nb, LANES), 1), 1.0, 0.0)
    sel = _mm_tn(sel_t, eye)

    scale = dh ** -0.5
    qbf = q.astype(BF16)
    selbf = sel.astype(BF16)
    half = BS // 2

    def fold(t):
        return jnp.maximum(t[:, 0:half], t[:, half:BS])

    qi = lax.broadcasted_iota(jnp.int32, (BS, BS), 0)
    ki = lax.broadcasted_iota(jnp.int32, (BS, BS), 1)
    own0 = pl.multiple_of(qb * BS, BS)
    s = _mm_nt(qbf, k_ref[pl.ds(own0, BS), :]) * scale + tiles_ref[0]
    s = jnp.where(ki <= qi, s, -jnp.inf)
    s_ref[qb] = s
    mx_ref[...] = fold(s)
    blk = lax.broadcasted_iota(jnp.int32, (LANES, BS), 0)

    def logits(js):
        raw = [_mm_nt(qbf, k_ref[pl.ds(pl.multiple_of(j * BS, BS), BS), :]) for j in js]
        chosen = [jnp.dot(selbf, jnp.where(blk == j, 1.0, 0.0).astype(BF16),
                          preferred_element_type=F32) for j in js]
        mx = mx_ref[...]
        for j, r, c in zip(js, raw, chosen):
            s = jnp.where(c > 0.5, r * scale + tiles_ref[qb - j], -jnp.inf)
            s_ref[j] = s
            mx = jnp.maximum(mx, fold(s))
        mx_ref[...] = mx

    def pairs(n, fn):
        def body(jj, _):
            fn([2 * jj, 2 * jj + 1])
            return 0

        lax.fori_loop(0, n // 2, body, 0)

        @pl.when(n % 2 == 1)
        def _():
            fn([n - 1])

    pairs(qb, logits)
    m = jnp.max(mx_ref[...], axis=1, keepdims=True)

    ls_ref[...] = jnp.zeros_like(ls_ref)
    acc_ref[...] = jnp.zeros_like(acc_ref)

    def accumulate(js):
        ps = [jnp.exp(s_ref[j] - m) for j in js]
        pv = [_mm(p, v_ref[pl.ds(pl.multiple_of(j * BS, BS), BS), :]) for j, p in zip(js, ps)]
        ls, acc = ls_ref[...], acc_ref[...]
        for p, o in zip(ps, pv):
            ls = ls + (p[:, 0:half] + p[:, half:BS])
            acc = acc + o
        ls_ref[...] = ls
        acc_ref[...] = acc

    pairs(qb + 1, accumulate)
    o_ref[...] = acc_ref[...] / jnp.sum(ls_ref[...], axis=1, keepdims=True)


def _moba_mixer(x, g, w_in, rel_table, w_out, bsz):
    T, D = x.shape
    L = T // bsz
    H, dh, BS = MOBA_HEADS, LANES, MOBA_BLOCK
    nb = L // BS
    proj = _rms_matmul(x, g, w_in, 512).reshape(bsz, L, 3 * D)
    rev_len = L + BS
    dist = np.maximum(L - 1 - np.arange(rev_len), 0)
    bucket_rev = jnp.asarray(_t5_bucket_np(dist)[None, :])
    bias_rev = pl.pallas_call(
        _bias_vec_kernel,
        out_shape=jax.ShapeDtypeStruct((H, rev_len), F32),
        name="moba_bias",
    )(bucket_rev, rel_table.T).reshape(H, 1, rev_len)
    o = pl.pallas_call(
        functools.partial(_moba_kernel, L=L),
        out_shape=jax.ShapeDtypeStruct((bsz, L, D), F32),
        grid=(H, bsz, nb),
        in_specs=[pl.BlockSpec((None, BS, dh), lambda h, b, n: (b, n, h)),
                  pl.BlockSpec((None, L, dh), lambda h, b, n: (b, 0, H + h)),
                  pl.BlockSpec((None, L, dh), lambda h, b, n: (b, 0, 2 * H + h)),
                  pl.BlockSpec((None, 1, rev_len), lambda h, b, n: (h, 0, 0))],
        out_specs=pl.BlockSpec((None, BS, dh), lambda h, b, n: (b, n, h)),
        scratch_shapes=[pltpu.VMEM((LANES, dh), F32),
                        pltpu.VMEM((nb, BS, BS), F32),
                        pltpu.VMEM((nb, BS, BS), F32),
                        pltpu.VMEM((BS, BS // 2), F32),
                        pltpu.VMEM((BS, BS // 2), F32),
                        pltpu.VMEM((BS, dh), F32)],
        compiler_params=_cparams(("parallel", "arbitrary", "arbitrary")),
        name="moba",
    )(proj, proj, proj, bias_rev)
    return _matmul_residual(o.reshape(T, D), w_out, x)


def kernel(x, norm_g, final_norm_g, ffn_w_gate, ffn_w_up, ffn_w_down, s5_lam_re, s5_lam_im, s5_log_dt, s5_b_re, s5_b_im, s5_c_re, s5_c_im, s5_d, s5_w_glu, hg_w_in, hg_lower_bounds, hg_norm_g, hg_w_out, gdn_w_in, gdn_conv_w, gdn_a_log, gdn_dt_bias, gdn_norm_g, gdn_w_out, moba_w_in, moba_w_out, rel_bias_table):
    bsz, L, D = x.shape
    depth = norm_g.shape[0]
    h = x.reshape(bsz * L, D)
    for i in range(depth):
        m, j = i % 4, i // 4
        h = _ffn(h, norm_g[i, 0], ffn_w_gate[i, 0], ffn_w_up[i, 0], ffn_w_down[i, 0])
        if m == 0:
            h = _s5_mixer(h, norm_g[i, 1], s5_lam_re[j], s5_lam_im[j], s5_log_dt[j], s5_b_re[j],
                          s5_b_im[j], s5_c_re[j], s5_c_im[j], s5_d[j], s5_w_glu[j], bsz)
        elif m == 1:
            h = _hgrn_mixer(h, norm_g[i, 1], hg_w_in[j], hg_lower_bounds, hg_norm_g[j], hg_w_out[j],
                            bsz, i)
        elif m == 2:
            h = _gdn_mixer(h, norm_g[i, 1], gdn_w_in[j], gdn_conv_w[j], gdn_a_log[j], gdn_dt_bias[j],
                           gdn_norm_g[j], gdn_w_out[j], bsz)
        else:
            h = _moba_mixer(h, norm_g[i, 1], moba_w_in[j], rel_bias_table, moba_w_out[j], bsz)
        h = _ffn(h, norm_g[i, 2], ffn_w_gate[i, 1], ffn_w_up[i, 1], ffn_w_down[i, 1])
    return _rms(h, final_norm_g).reshape(bsz, L, D)
```

```python
import functools
import math

import jax
import jax.numpy as jnp
import numpy as np
from jax import lax
from jax.experimental import pallas as pl
from jax.experimental.pallas import tpu as pltpu

F32 = jnp.float32
BF16 = jnp.bfloat16
LANES = 128
RMS_EPS = 1e-6

HG_HEADS = 8
HG_CHUNK = 32
GDN_QK_HEADS = 8
GDN_V_HEADS = 16
GDN_CHUNK = 64
GDN_CONV = 4
MOBA_HEADS = 8
MOBA_BLOCK = 256
MOBA_TOPK = 3
REL_BUCKETS = 32
REL_MAX_DIST = 2048
S5_GROUP = 16
S5_STATE = 64
S5_TC = 16
S5_GP = LANES // S5_GROUP

VMEM_LIMIT = 56 * 1024 * 1024


def _cparams(sem):
    return pltpu.CompilerParams(dimension_semantics=sem, vmem_limit_bytes=VMEM_LIMIT)


def _mm(a, b):
    return jnp.dot(a.astype(BF16), b.astype(BF16), preferred_element_type=F32)


def _mm_nt(a, b):
    return lax.dot_general(a.astype(BF16), b.astype(BF16), (((1,), (1,)), ((), ())),
                           preferred_element_type=F32)


def _mm_tn(a, b):
    return lax.dot_general(a.astype(BF16), b.astype(BF16), (((0,), (0,)), ((), ())),
                           preferred_element_type=F32)


def _mm_hi(a, b):
    return jnp.dot(a, b, precision=lax.Precision.HIGHEST, preferred_element_type=F32)


def _mm_nt_hi(a, b):
    return lax.dot_general(a, b, (((1,), (1,)), ((), ())), precision=lax.Precision.HIGHEST,
                           preferred_element_type=F32)


def _sigmoid(x):
    return 1.0 / (1.0 + jnp.exp(-x))


def _silu(x):
    return x * _sigmoid(x)


def _rms_rows(x, g):
    return x * lax.rsqrt(jnp.mean(x * x, axis=-1, keepdims=True) + RMS_EPS) * g


def _ffn_kernel(x_ref, g_ref, wg_ref, wu_ref, wd_ref, o_ref, h_ref, acc_ref):
    j = pl.program_id(1)

    @pl.when(j == 0)
    def _():
        h_ref[...] = _rms_rows(x_ref[...], g_ref[...]).astype(BF16)
        acc_ref[...] = jnp.zeros_like(acc_ref)

    h = h_ref[...]
    gate = jnp.dot(h, wg_ref[...].astype(BF16), preferred_element_type=F32)
    up = jnp.dot(h, wu_ref[...].astype(BF16), preferred_element_type=F32)
    a = (_silu(gate) * up).astype(BF16)
    acc_ref[...] += jnp.dot(a, wd_ref[...].astype(BF16), preferred_element_type=F32)

    @pl.when(j == pl.num_programs(1) - 1)
    def _():
        o_ref[...] = x_ref[...] + 0.5 * acc_ref[...]


def _ffn(x, g, wg, wu, wd):
    T, D = x.shape
    F = wg.shape[1]
    tm = min(1024, T)
    tf = 256
    return pl.pallas_call(
        _ffn_kernel,
        out_shape=jax.ShapeDtypeStruct((T, D), F32),
        grid=(T // tm, F // tf),
        in_specs=[pl.BlockSpec((tm, D), lambda i, j: (i, 0)),
                  pl.BlockSpec((1, D), lambda i, j: (0, 0)),
                  pl.BlockSpec((D, tf), lambda i, j: (0, j)),
                  pl.BlockSpec((D, tf), lambda i, j: (0, j)),
                  pl.BlockSpec((tf, D), lambda i, j: (j, 0))],
        out_specs=pl.BlockSpec((tm, D), lambda i, j: (i, 0)),
        scratch_shapes=[pltpu.VMEM((tm, D), BF16), pltpu.VMEM((tm, D), F32)],
        compiler_params=_cparams(("parallel", "arbitrary")),
        name="ffn",
    )(x, g.reshape(1, D), wg, wu, wd)


def _rms_kernel(x_ref, g_ref, o_ref):
    o_ref[...] = _rms_rows(x_ref[...], g_ref[...])


def _rms(x, g):
    T, D = x.shape
    tm = min(1024, T)
    return pl.pallas_call(
        _rms_kernel,
        out_shape=jax.ShapeDtypeStruct((T, D), F32),
        grid=(T // tm,),
        in_specs=[pl.BlockSpec((tm, D), lambda i: (i, 0)), pl.BlockSpec((1, D), lambda i: (0, 0))],
        out_specs=pl.BlockSpec((tm, D), lambda i: (i, 0)),
        compiler_params=_cparams(("parallel",)),
        name="rms",
    )(x, g.reshape(1, D))


def _rmsmm_kernel(x_ref, g_ref, w_ref, o_ref, h_ref):
    @pl.when(pl.program_id(1) == 0)
    def _():
        h_ref[...] = _rms_rows(x_ref[...], g_ref[...]).astype(BF16)

    o_ref[...] = jnp.dot(h_ref[...], w_ref[...].astype(BF16), preferred_element_type=F32)


def _rms_matmul(x, g, w, tn):
    T, D = x.shape
    N = w.shape[1]
    tm = min(1024, T)
    return pl.pallas_call(
        _rmsmm_kernel,
        out_shape=jax.ShapeDtypeStruct((T, N), F32),
        grid=(T // tm, N // tn),
        in_specs=[pl.BlockSpec((tm, D), lambda i, j: (i, 0)),
                  pl.BlockSpec((1, D), lambda i, j: (0, 0)),
                  pl.BlockSpec((D, tn), lambda i, j: (0, j))],
        out_specs=pl.BlockSpec((tm, tn), lambda i, j: (i, j)),
        scratch_shapes=[pltpu.VMEM((tm, D), BF16)],
        compiler_params=_cparams(("parallel", "arbitrary")),
        name="rms_matmul",
    )(x, g.reshape(1, D), w)


def _mmres_kernel(a_ref, w_ref, x_ref, o_ref):
    o_ref[...] = x_ref[...] + _mm(a_ref[...], w_ref[...])


def _matmul_residual(a, w, x):
    T, K = a.shape
    N = w.shape[1]
    tm = min(512, T)
    return pl.pallas_call(
        _mmres_kernel,
        out_shape=jax.ShapeDtypeStruct((T, N), F32),
        grid=(T // tm,),
        in_specs=[pl.BlockSpec((tm, K), lambda i: (i, 0)),
                  pl.BlockSpec((K, N), lambda i: (0, 0)),
                  pl.BlockSpec((tm, N), lambda i: (i, 0))],
        out_specs=pl.BlockSpec((tm, N), lambda i: (i, 0)),
        compiler_params=_cparams(("parallel",)),
        name="matmul_residual",
    )(a, w, x)


def _cmul(ar, ai, br, bi):
    return ar * br - ai * bi, ar * bi + ai * br


def _s5_powers(lr, li, ldt, n):
    lr = jnp.minimum(lr, -1e-4)
    dt = jnp.exp(ldt)
    mag = jnp.exp(lr * dt)
    br = mag * jnp.cos(li * dt)
    bi = mag * jnp.sin(li * dt)
    den = lr * lr + li * li
    fr = ((br - 1.0) * lr + bi * li) / den
    fi = (bi * lr - (br - 1.0) * li) / den
    pw = [(jnp.ones_like(br), jnp.zeros_like(bi))]
    for _ in range(n):
        pw.append(_cmul(pw[-1][0], pw[-1][1], br, bi))
    return pw, (fr, fi)


def _s5_prep_kernel(lr_row, li_row, ldt_row, lr_col, li_col, ldt_col, btr_ref, bti_ref, ctr_ref,
                    cti_ref, bd_ref, bexp_ref, cexp_ref, a_ref):
    tc = S5_TC
    ns = lr_row.shape[-1]
    pw_row, (fr, fi) = _s5_powers(lr_row[...], li_row[...], ldt_row[...], tc)
    pw_col, _ = _s5_powers(lr_col[...], li_col[...], ldt_col[...], tc)
    btr, bti = btr_ref[...], bti_ref[...]
    ctr, cti = ctr_ref[...], cti_ref[...]
    for k in range(tc):
        fer, fei = _cmul(fr, fi, pw_row[k][0], pw_row[k][1])
        xr = btr * fer - bti * fei
        xi = btr * fei + bti * fer
        bd_ref[k] = (_mm_hi(xr, ctr) - _mm_hi(xi, cti)).astype(BF16)
        s = tc - 1 - k
        bexp_ref[s * LANES:(s + 1) * LANES, 0:ns] = xr.astype(BF16)
        bexp_ref[s * LANES:(s + 1) * LANES, ns:2 * ns] = xi.astype(BF16)
        er, ei = pw_col[k + 1]
        cexp_ref[0:ns, k * LANES:(k + 1) * LANES] = (ctr * er - cti * ei).astype(BF16)
        cexp_ref[ns:2 * ns, k * LANES:(k + 1) * LANES] = (-(ctr * ei + cti * er)).astype(BF16)
    a_ref[:, 0:ns] = pw_row[tc][0]
    a_ref[:, ns:2 * ns] = pw_row[tc][1]


def _s5_main_kernel(u_ref, bd_ref, bexp_ref, cexp_ref, a_ref, y_ref, m_ref, hl_ref, hp_ref):
    tc = S5_TC
    ns = a_ref.shape[-1] // 2
    rows = u_ref.shape[0]

    @pl.when(pl.program_id(1) == 0)
    def _():
        for s in range(tc):
            for t in range(tc):
                blk = bd_ref[t - s] if t >= s else jnp.zeros((LANES, LANES), BF16)
                m_ref[s * LANES:(s + 1) * LANES, t * LANES:(t + 1) * LANES] = blk

    u = u_ref[...].astype(BF16)
    hl_ref[...] = jnp.dot(u, bexp_ref[...], preferred_element_type=F32)
    are, aim = a_ref[:, 0:ns], a_ref[:, ns:2 * ns]

    def step(c, carry):
        hre, him = carry
        hp_ref[pl.ds(c, 1), 0:ns] = hre
        hp_ref[pl.ds(c, 1), ns:2 * ns] = him
        xre = hl_ref[pl.ds(c, 1), 0:ns]
        xim = hl_ref[pl.ds(c, 1), ns:2 * ns]
        return are * hre - aim * him + xre, are * him + aim * hre + xim

    z = jnp.zeros((1, ns), F32)
    lax.fori_loop(0, rows, step, (z, z))
    y_ref[...] = (jnp.dot(u, m_ref[...], preferred_element_type=F32)
                  + jnp.dot(hp_ref[...].astype(BF16), cexp_ref[...], preferred_element_type=F32))


def _s5_pre_kernel(x_ref, g_ref, ug_ref, u_ref):
    tc = S5_TC
    ng, r, _ = ug_ref.shape
    u = _rms_rows(x_ref[...], g_ref[...])
    for n in range(ng):
        u_ref[n] = u[:, n * LANES:(n + 1) * LANES]
    for n in range(ng):
        for t in range(tc):
            ug_ref[n, :, t * LANES:(t + 1) * LANES] = (
                u_ref[n, pl.ds(t, r, stride=tc), :].astype(BF16))


def _s5_post_kernel(yg_ref, x_ref, g_ref, d_ref, wv_ref, wg_ref, o_ref, a_ref, y_ref):
    tc = S5_TC
    ng, r, _ = yg_ref.shape
    tn = o_ref.shape[1]
    j = pl.program_id(1)

    @pl.when(j == 0)
    def _():
        for n in range(ng):
            for t in range(tc):
                y_ref[n, pl.ds(t, r, stride=tc), :] = yg_ref[n, :, t * LANES:(t + 1) * LANES]
        du = d_ref[...] * _rms_rows(x_ref[...], g_ref[...])
        c = math.sqrt(2.0 / math.pi)
        for n in range(ng):
            y = y_ref[n] + du[:, n * LANES:(n + 1) * LANES]
            a_ref[:, n * LANES:(n + 1) * LANES] = (
                0.5 * y * (1.0 + jnp.tanh(c * (y + 0.044715 * (y * y * y))))).astype(BF16)

    a = a_ref[...]
    val = jnp.dot(a, wv_ref[...].astype(BF16), preferred_element_type=F32)
    gate = jnp.dot(a, wg_ref[...].astype(BF16), preferred_element_type=F32)
    o_ref[...] = x_ref[:, pl.ds(pl.multiple_of(j * tn, tn), tn)] + val * _sigmoid(gate)


def _s5_mixer(x, g, lam_re, lam_im, log_dt, b_re, b_im, c_re, c_im, d_skip, w_glu, bsz):
    T, D = x.shape
    L = T // bsz
    G, P = lam_re.shape
    tc, gp = S5_TC, S5_GP
    ng = G // gp
    ns = gp * P
    rows = L // tc

    eye = jnp.eye(gp, dtype=F32)

    def bt(b):
        return jnp.einsum('ngph,gk->nghkp', b.reshape(ng, gp, P, S5_GROUP), eye).reshape(ng, LANES, ns)

    def ct(c):
        return jnp.einsum('nghp,gk->ngpkh', c.reshape(ng, gp, S5_GROUP, P), eye).reshape(ng, ns, LANES)

    lr, li = lam_re.reshape(ng, 1, ns), lam_im.reshape(ng, 1, ns)
    ldt = jnp.broadcast_to(log_dt[:, None], (G, P)).reshape(ng, 1, ns)
    row_spec = pl.BlockSpec((None, 1, ns), lambda n: (n, 0, 0))
    col_spec = pl.BlockSpec((None, ns, 1), lambda n: (n, 0, 0))
    bd, bexp, cexp, a16 = pl.pallas_call(
        _s5_prep_kernel,
        out_shape=(jax.ShapeDtypeStruct((ng, tc, LANES, LANES), BF16),
                   jax.ShapeDtypeStruct((ng, tc * LANES, 2 * ns), BF16),
                   jax.ShapeDtypeStruct((ng, 2 * ns, tc * LANES), BF16),
                   jax.ShapeDtypeStruct((ng, 1, 2 * ns), F32)),
        grid=(ng,),
        in_specs=[row_spec, row_spec, row_spec, col_spec, col_spec, col_spec,
                  pl.BlockSpec((None, LANES, ns), lambda n: (n, 0, 0)),
                  pl.BlockSpec((None, LANES, ns), lambda n: (n, 0, 0)),
                  pl.BlockSpec((None, ns, LANES), lambda n: (n, 0, 0)),
                  pl.BlockSpec((None, ns, LANES), lambda n: (n, 0, 0))],
        out_specs=(pl.BlockSpec((None, tc, LANES, LANES), lambda n: (n, 0, 0, 0)),
                   pl.BlockSpec((None, tc * LANES, 2 * ns), lambda n: (n, 0, 0)),
                   pl.BlockSpec((None, 2 * ns, tc * LANES), lambda n: (n, 0, 0)),
                   pl.BlockSpec((None, 1, 2 * ns), lambda n: (n, 0, 0))),
        compiler_params=_cparams(("parallel",)),
        name="s5_prep",
    )(lr, li, ldt, lr.reshape(ng, ns, 1), li.reshape(ng, ns, 1), ldt.reshape(ng, ns, 1),
      bt(b_re), bt(b_im), ct(c_re), ct(c_im))

    tm = min(1024, T)
    ug = pl.pallas_call(
        _s5_pre_kernel,
        out_shape=jax.ShapeDtypeStruct((ng, T // tc, tc * LANES), BF16),
        grid=(T // tm,),
        in_specs=[pl.BlockSpec((tm, D), lambda i: (i, 0)), pl.BlockSpec((1, D), lambda i: (0, 0))],
        out_specs=pl.BlockSpec((ng, tm // tc, tc * LANES), lambda i: (0, i, 0)),
        scratch_shapes=[pltpu.VMEM((ng, tm, LANES), F32)],
        compiler_params=_cparams(("parallel",)),
        name="s5_pre",
    )(x, g.reshape(1, D))
    yg = pl.pallas_call(
        _s5_main_kernel,
        out_shape=jax.ShapeDtypeStruct((ng, T // tc, tc * LANES), F32),
        grid=(ng, bsz),
        in_specs=[pl.BlockSpec((None, rows, tc * LANES), lambda n, b: (n, b, 0)),
                  pl.BlockSpec((None, tc, LANES, LANES), lambda n, b: (n, 0, 0, 0)),
                  pl.BlockSpec((None, tc * LANES, 2 * ns), lambda n, b: (n, 0, 0)),
                  pl.BlockSpec((None, 2 * ns, tc * LANES), lambda n, b: (n, 0, 0)),
                  pl.BlockSpec((None, 1, 2 * ns), lambda n, b: (n, 0, 0))],
        out_specs=pl.BlockSpec((None, rows, tc * LANES), lambda n, b: (n, b, 0)),
        scratch_shapes=[pltpu.VMEM((tc * LANES, tc * LANES), BF16),
                        pltpu.VMEM((rows, 2 * ns), F32),
                        pltpu.VMEM((rows, 2 * ns), F32)],
        compiler_params=_cparams(("parallel", "arbitrary")),
        name="s5_main",
    )(ug, bd, bexp, cexp, a16)

    tn = 512
    nj = D // tn
    return pl.pallas_call(
        _s5_post_kernel,
        out_shape=jax.ShapeDtypeStruct((T, D), F32),
        grid=(T // tm, nj),
        in_specs=[pl.BlockSpec((ng, tm // tc, tc * LANES), lambda i, j: (0, i, 0)),
                  pl.BlockSpec((tm, D), lambda i, j: (i, 0)),
                  pl.BlockSpec((1, D), lambda i, j: (0, 0)),
                  pl.BlockSpec((1, D), lambda i, j: (0, 0)),
                  pl.BlockSpec((D, tn), lambda i, j: (0, j)),
                  pl.BlockSpec((D, tn), lambda i, j: (0, j + nj))],
        out_specs=pl.BlockSpec((tm, tn), lambda i, j: (i, j)),
        scratch_shapes=[pltpu.VMEM((tm, D), BF16), pltpu.VMEM((ng, tm, LANES), F32)],
        compiler_params=_cparams(("parallel", "arbitrary")),
        name="s5_post",
    )(yg, x, g.reshape(1, D), d_skip.reshape(1, D), w_glu, w_glu)


HG_SUPER = 256


def _hgrn_kernel(q_ref, f_ref, i_ref, g_ref, lb_ref, ng_ref, o_ref, st_ref, *, layer_idx):
    L, dk = q_ref.shape
    C, SC = HG_CHUNK, HG_SUPER
    lbraw = lb_ref[...]
    e = jnp.exp(lbraw - jnp.max(lbraw, axis=0, keepdims=True))
    sm = e / jnp.sum(e, axis=0, keepdims=True)
    layer = lax.broadcasted_iota(jnp.int32, sm.shape, 0)
    lb = jnp.sum(jnp.where((layer >= 1) & (layer <= layer_idx), sm, 0.0), axis=0, keepdims=True)

    ti = lax.broadcasted_iota(jnp.int32, (SC, SC), 0)
    si = lax.broadcasted_iota(jnp.int32, (SC, SC), 1)
    causal = ((ti // C) == (si // C)) & (si <= ti)
    pos = lax.broadcasted_iota(jnp.int32, (SC, dk), 0) & (C - 1)
    ng = ng_ref[...]
    st_ref[...] = jnp.zeros_like(st_ref)

    def body(sc, _):
        r0 = pl.multiple_of(sc * SC, SC)
        f = f_ref[pl.ds(r0, SC), :]
        fgate = lb + (1.0 - lb) * _sigmoid(f)
        logf = jnp.log(fgate)
        k = (1.0 - lb) * _sigmoid(-f)
        q = _silu(q_ref[pl.ds(r0, SC), :]) * dk ** -0.5
        v = i_ref[pl.ds(r0, SC), :]
        b, rest = logf, logf
        shift = 1
        while shift < C:
            b = b + jnp.where(pos >= shift, pltpu.roll(b, shift, 0), 0.0)
            rest = rest + jnp.where(pos + shift < C, pltpu.roll(rest, SC - shift, 0), 0.0)
            shift *= 2
        q_t = q * jnp.exp(b)
        k_t = k * jnp.exp(-b)
        k_state = k * jnp.exp(rest - logf)
        attn = jnp.where(causal, _mm_nt(q_t, k_t), 0.0)
        o = _mm(attn, v)
        nc = SC // C
        outer = [_mm_tn(v[c * C:(c + 1) * C], k_state[c * C:(c + 1) * C]) for c in range(nc)]
        states = [st_ref[...]]
        for c in range(nc):
            dc = jnp.exp(b[(c + 1) * C - 1:(c + 1) * C, :])
            states.append(states[c] * dc + outer[c])
        st_ref[...] = states[nc]
        inter = [_mm_nt(q_t[c * C:(c + 1) * C], states[c]) for c in range(nc)]
        o = o + jnp.concatenate(inter, axis=0)
        o = _rms_rows(o, ng) * _silu(g_ref[pl.ds(r0, SC), :])
        o_ref[pl.ds(r0, SC), :] = o
        return 0

    lax.fori_loop(0, L // SC, body, 0)


def _hgrn_mixer(x, g, w_in, lower_bounds, norm_g, w_out, bsz, layer_idx):
    T, D = x.shape
    L = T // bsz
    H = HG_HEADS
    dk = D // H
    proj = _rms_matmul(x, g, w_in, 512).reshape(bsz, L, 4 * D)
    nl = lower_bounds.shape[0]

    def col(off):
        return pl.BlockSpec((None, L, dk), lambda b, h: (b, 0, off * H + h))

    o = pl.pallas_call(
        functools.partial(_hgrn_kernel, layer_idx=layer_idx),
        out_shape=jax.ShapeDtypeStruct((bsz, L, D), F32),
        grid=(bsz, H),
        in_specs=[col(0), col(1), col(2), col(3),
                  pl.BlockSpec((nl, dk), lambda b, h: (0, h)),
                  pl.BlockSpec((1, dk), lambda b, h: (0, 0))],
        out_specs=pl.BlockSpec((None, L, dk), lambda b, h: (b, 0, h)),
        scratch_shapes=[pltpu.VMEM((dk, dk), F32)],
        compiler_params=_cparams(("parallel", "parallel")),
        name="hgrn",
    )(proj, proj, proj, proj, lower_bounds, norm_g.reshape(1, dk))
    return _matmul_residual(o.reshape(T, D), w_out, x)


GDN_LTILE = 512
GDN_QK_PER_STEP = 4
GDN_SOLVE_CHUNKS = 2
GDN_SOLVE_WIDE_STEPS = 1


def _split3(x):
    x1 = x.astype(BF16)
    r = x - x1.astype(F32)
    x2 = r.astype(BF16)
    return x1, x2, (r - x2.astype(F32)).astype(BF16)


def _mm_sel(x, sel):
    s = sel.astype(BF16)
    p1, p2, p3 = (jnp.dot(p, s, preferred_element_type=F32) for p in _split3(x))
    return p1 + (p2 + p3)


def _mm_sel_nt(sel, x):
    s = sel.astype(BF16)
    p1, p2, p3 = (lax.dot_general(s, p, (((1,), (1,)), ((), ())), preferred_element_type=F32)
                  for p in _split3(x))
    return p1 + (p2 + p3)


def _mm3(a, b):
    a1 = a.astype(BF16)
    a2 = (a - a1.astype(F32)).astype(BF16)
    b1 = b.astype(BF16)
    b2 = (b - b1.astype(F32)).astype(BF16)
    return (jnp.dot(a1, b1, preferred_element_type=F32)
            + (jnp.dot(a1, b2, preferred_element_type=F32) + jnp.dot(a2, b1, preferred_element_type=F32)))


def _gdn_kernel(q_ref, k_ref, v_ref, z_ref, gc_ref, cwq_ref, cwk_ref, cwv_ref, alog_ref, dtb_ref,
                ng_ref, o_ref, xq_ref, xk_ref, xv_ref, qn_ref, kn_ref, vc_ref, bcol_ref, gcol_ref,
                beta_ref, bcum_ref, u_ref, w_ref, qk_ref, s_ref):
    Lt = q_ref.shape[0]
    dk = LANES
    C = GDN_CHUNK
    nqk = GDN_QK_PER_STEP
    nh = nqk * (GDN_V_HEADS // GDN_QK_HEADS)
    hg = pl.program_id(1)
    first = pl.program_id(2) == 0
    pad = 8

    @pl.when(first)
    def _():
        s_ref[...] = jnp.zeros_like(s_ref)
        xq_ref[0:pad, :] = jnp.zeros((pad, xq_ref.shape[1]), F32)
        xk_ref[0:pad, :] = jnp.zeros((pad, xk_ref.shape[1]), F32)
        xv_ref[0:pad, :] = jnp.zeros((pad, xv_ref.shape[1]), F32)

    def conv_silu(src_ref, w_ref, xp_ref):
        xp_ref[pad:pad + Lt, :] = src_ref[...]
        w = w_ref[...]
        acc = None
        for j in range(GDN_CONV):
            s = pad - (GDN_CONV - 1) + j
            term = w[j:j + 1, :] * xp_ref[s:s + Lt, :]
            acc = term if acc is None else acc + term
        xp_ref[0:pad, :] = xp_ref[Lt:Lt + pad, :]
        return _silu(acc)

    def l2n(t):
        return t * lax.rsqrt(jnp.sum(t * t, axis=-1, keepdims=True) + 1e-6)

    qa = conv_silu(q_ref, cwq_ref, xq_ref)
    ka = conv_silu(k_ref, cwk_ref, xk_ref)
    for i in range(nqk):
        qn_ref[:, i * dk:(i + 1) * dk] = l2n(qa[:, i * dk:(i + 1) * dk]) * dk ** -0.5
        kn_ref[:, i * dk:(i + 1) * dk] = l2n(ka[:, i * dk:(i + 1) * dk])
    vc_ref[...] = conv_silu(v_ref, cwv_ref, xv_ref)

    gcraw = gc_ref[...]
    bcol_ref[...] = _sigmoid(gcraw)
    xs = gcraw + dtb_ref[...]
    softplus = jnp.maximum(xs, 0.0) + jnp.log(1.0 + jnp.exp(-jnp.abs(xs)))
    gcol_ref[...] = -jnp.exp(alog_ref[...]) * softplus

    row = lax.broadcasted_iota(jnp.int32, (Lt, LANES), 0) & (C - 1)
    bc_all = gcol_ref[...]
    shift = 1
    while shift < C:
        bc_all = bc_all + jnp.where(row >= shift, pltpu.roll(bc_all, shift, 0), 0.0)
        shift *= 2
    sel_r = lax.broadcasted_iota(jnp.int32, (LANES, LANES), 0)
    braw = bcol_ref[...]
    for j in range(nh):
        hv = nh * hg + j
        beta_ref[j] = _mm_sel(braw, jnp.where(sel_r == hv, 1.0, 0.0))
        bcum_ref[j] = _mm_sel(bc_all, jnp.where(sel_r == hv + GDN_V_HEADS, 1.0, 0.0))

    ti = lax.broadcasted_iota(jnp.int32, (C, C), 0)
    si = lax.broadcasted_iota(jnp.int32, (C, C), 1)
    incl = si <= ti
    strict = si < ti
    eye_c = jnp.where(si == ti, 1.0, 0.0)
    ng = ng_ref[...]
    rep = nh // nqk

    def solve_body(cc, _):
        rows = [pl.ds(pl.multiple_of((cc * GDN_SOLVE_CHUNKS + a) * C, C), C)
                for a in range(GDN_SOLVE_CHUNKS)]
        kk, qk0 = {}, {}
        for a, rw in enumerate(rows):
            for i in range(nqk):
                kc = kn_ref[rw, i * dk:(i + 1) * dk]
                kk[a, i] = _mm_nt(kc, kc)
                qk0[a, i] = _mm_nt(qn_ref[rw, i * dk:(i + 1) * dk], kc)
        probs = [(a, j) for a in range(GDN_SOLVE_CHUNKS) for j in range(nh)]
        pmat, sol, decay = {}, {}, {}
        for a, j in probs:
            i = j // rep
            bcum = bcum_ref[j, rows[a], :]
            beta = beta_ref[j, rows[a], :]
            dmat = bcum[:, 0:C] - jnp.transpose(bcum)[0:1, :]
            decay[a, j] = jnp.where(incl, jnp.exp(jnp.where(incl, dmat, 0.0)), 0.0)
            pmat[a, j] = jnp.where(strict, -(kk[a, i] * beta[:, 0:C]) * decay[a, j], 0.0)
            sol[a, j] = jnp.concatenate(
                [vc_ref[rows[a], j * dk:(j + 1) * dk] * beta,
                 kn_ref[rows[a], i * dk:(i + 1) * dk] * beta * jnp.exp(bcum)], axis=1)
        inv = {p: eye_c + pmat[p] for p in probs}
        pmat = {p: _mm3(pmat[p], pmat[p]) for p in probs}
        for it in range(1, 6):
            mm = _mm3 if it < GDN_SOLVE_WIDE_STEPS else _mm
            if it < 5:
                prod = {p: mm(pmat[p], jnp.concatenate([inv[p], pmat[p]], axis=1)) for p in probs}
                inv = {p: inv[p] + prod[p][:, 0:C] for p in probs}
                pmat = {p: prod[p][:, C:2 * C] for p in probs}
            else:
                inv = {p: inv[p] + mm(pmat[p], inv[p]) for p in probs}
        sol = {p: _mm3(inv[p], sol[p]) for p in probs}
        for a, j in probs:
            u_ref[j, rows[a], :] = sol[a, j][:, 0:dk]
            w_ref[j, rows[a], :] = sol[a, j][:, dk:2 * dk]
            qk_ref[j, rows[a], :] = qk0[a, j // rep] * decay[a, j]
        return 0

    lax.fori_loop(0, Lt // (C * GDN_SOLVE_CHUNKS), solve_body, 0)

    def state_body(c, _):
        r0 = pl.multiple_of(c * C, C)
        rows = pl.ds(r0, C)
        heads = range(nh)
        S, blast, k_state, ws = [], [], [], []
        for j in heads:
            i = j // rep
            bcum = bcum_ref[j, rows, :]
            blast.append(bcum_ref[j, pl.ds(r0 + C - 1, 1), :])
            q_dec = qn_ref[rows, i * dk:(i + 1) * dk] * jnp.exp(bcum)
            k_state.append(kn_ref[rows, i * dk:(i + 1) * dk] * jnp.exp(blast[j] - bcum))
            S.append(s_ref[j])
            ws.append(_mm(jnp.concatenate([w_ref[j, rows, :], q_dec], axis=0), S[j]))
        v_new = [u_ref[j, rows, :] - ws[j][0:C] for j in heads]
        for j in heads:
            s_ref[j] = S[j] * jnp.exp(blast[j]) + _mm_tn(k_state[j], v_new[j])
        for j in heads:
            o = ws[j][C:2 * C] + _mm(qk_ref[j, rows, :], v_new[j])
            z = z_ref[rows, j * dk:(j + 1) * dk]
            o_ref[rows, j * dk:(j + 1) * dk] = _rms_rows(o, ng) * _silu(z)
        return 0

    lax.fori_loop(0, Lt // C, state_body, 0)


def _gdn_mixer(x, g, w_in, conv_w, a_log, dt_bias, norm_g, w_out, bsz):
    T, D = x.shape
    L = T // bsz
    Hk, Hv = GDN_QK_HEADS, GDN_V_HEADS
    dk = LANES
    nqk = GDN_QK_PER_STEP
    nh = nqk * (Hv // Hk)
    ngrp = Hk // nqk
    qw, vw = nqk * dk, nh * dk
    lt = min(GDN_LTILE, L)
    n_in = w_in.shape[1]
    n_pad = -(-n_in // (7 * LANES)) * (7 * LANES)
    w_pad = jnp.pad(w_in, ((0, 0), (0, n_pad - n_in)))
    proj = _rms_matmul(x, g, w_pad, 7 * LANES).reshape(bsz, L, n_pad)
    qkv_w = 2 * Hk * dk + Hv * dk
    gate_blk = (qkv_w + Hv * dk) // LANES
    alog_row = jnp.zeros((1, LANES), F32).at[0, Hv:2 * Hv].set(a_log)
    dtb_row = jnp.zeros((1, LANES), F32).at[0, Hv:2 * Hv].set(dt_bias)
    o = pl.pallas_call(
        _gdn_kernel,
        out_shape=jax.ShapeDtypeStruct((bsz, L, Hv * dk), F32),
        grid=(bsz, ngrp, L // lt),
        in_specs=[pl.BlockSpec((None, lt, qw), lambda b, h, t: (b, t, h)),
                  pl.BlockSpec((None, lt, qw), lambda b, h, t: (b, t, ngrp + h)),
                  pl.BlockSpec((None, lt, vw), lambda b, h, t: (b, t, ngrp + h)),
                  pl.BlockSpec((None, lt, vw), lambda b, h, t: (b, t, 2 * ngrp + h)),
                  pl.BlockSpec((None, lt, LANES), lambda b, h, t: (b, t, gate_blk)),
                  pl.BlockSpec((GDN_CONV, qw), lambda b, h, t: (0, h)),
                  pl.BlockSpec((GDN_CONV, qw), lambda b, h, t: (0, ngrp + h)),
                  pl.BlockSpec((GDN_CONV, vw), lambda b, h, t: (0, ngrp + h)),
                  pl.BlockSpec((1, LANES), lambda b, h, t: (0, 0)),
                  pl.BlockSpec((1, LANES), lambda b, h, t: (0, 0)),
                  pl.BlockSpec((1, dk), lambda b, h, t: (0, 0))],
        out_specs=pl.BlockSpec((None, lt, vw), lambda b, h, t: (b, t, h)),
        scratch_shapes=[pltpu.VMEM((lt + 8, qw), F32),
                        pltpu.VMEM((lt + 8, qw), F32),
                        pltpu.VMEM((lt + 8, vw), F32),
                        pltpu.VMEM((lt, qw), F32),
                        pltpu.VMEM((lt, qw), F32),
                        pltpu.VMEM((lt, vw), F32),
                        pltpu.VMEM((lt, LANES), F32),
                        pltpu.VMEM((lt, LANES), F32),
                        pltpu.VMEM((nh, lt, LANES), F32),
                        pltpu.VMEM((nh, lt, LANES), F32),
                        pltpu.VMEM((nh, lt, dk), F32),
                        pltpu.VMEM((nh, lt, dk), F32),
                        pltpu.VMEM((nh, lt, GDN_CHUNK), F32),
                        pltpu.VMEM((nh, dk, dk), F32)],
        compiler_params=_cparams(("parallel", "parallel", "arbitrary")),
        name="gdn",
    )(proj, proj, proj, proj, proj, conv_w, conv_w, conv_w, alog_row, dtb_row, norm_g.reshape(1, dk))
    return _matmul_residual(o.reshape(T, Hv * dk), w_out, x)


def _t5_bucket_np(n):
    max_exact = REL_BUCKETS // 2
    nf = np.maximum(n, 1).astype(np.float32)
    large = max_exact + (np.log(nf / np.float32(max_exact)) / np.float32(math.log(REL_MAX_DIST / max_exact))
                         * np.float32(REL_BUCKETS - max_exact)).astype(np.int32)
    large = np.minimum(large, REL_BUCKETS - 1)
    return np.where(n < max_exact, n, large).astype(np.int32)


def _bias_vec_kernel(bucket_ref, table_ref, o_ref):
    bucket = bucket_ref[...]
    table = table_ref[...]
    acc = jnp.zeros(o_ref.shape, F32)
    for k in range(REL_BUCKETS):
        acc = acc + jnp.where(bucket == k, table[:, k:k + 1], 0.0)
    o_ref[...] = acc


MOBA_MASK = 1e30
MOBA_UNROLL = 4


def _moba_kernel(q_ref, k_ref, v_ref, bias_ref, o_ref, kmean_ref, mask_ref, tiles_ref, s_ref, mx_ref,
                 ls_ref, acc_ref, *, L):
    BS = MOBA_BLOCK
    dh = q_ref.shape[-1]
    nb = L // BS
    qb = pl.program_id(2)
    rev_len = bias_ref.shape[-1]

    @pl.when((pl.program_id(1) == 0) & (qb == 0))
    def _():
        for d in range(nb):
            start = rev_len - 2 * BS - d * BS
            wb = jnp.broadcast_to(bias_ref[:, start:start + 2 * BS], (BS, 2 * BS))
            tiles_ref[d] = pltpu.roll(wb, BS + 1, 1, stride=1, stride_axis=0)[:, 0:BS]

    @pl.when(qb == 0)
    def _():
        kmean_ref[...] = jnp.zeros_like(kmean_ref)
        for n in range(nb):
            kmean_ref[n:n + 1, :] = jnp.sum(k_ref[n * BS:(n + 1) * BS, :], axis=0, keepdims=True) / BS
        blk_row = lax.broadcasted_iota(jnp.int32, (nb, L), 0)
        q_blk = lax.broadcasted_iota(jnp.int32, (nb, L), 1) // BS
        past = blk_row < q_blk
        gate = jnp.where(past, _mm_nt_hi(kmean_ref[0:nb, :], q_ref[...]), -1e30)
        sel_t = jnp.zeros((nb, L), F32)
        for _ in range(MOBA_TOPK):
            mx = jnp.max(gate, axis=0, keepdims=True)
            first = jnp.min(jnp.where(gate == mx, blk_row, nb), axis=0, keepdims=True)
            pick = blk_row == first
            sel_t = jnp.where(pick & past, 1.0, sel_t)
            gate = jnp.where(pick, -jnp.inf, gate)
        eye = jnp.where(lax.broadcasted_iota(jnp.int32, (nb, LANES), 0)
                        == lax.broadcasted_iota(jnp.int32, (nb, LANES), 1), 1.0, 0.0)
        mask_ref[...] = _mm_tn((sel_t - 1.0) * MOBA_MASK, eye).astype(BF16)

    scale = dh ** -0.5
    own0 = pl.multiple_of(qb * BS, BS)
    qbf = q_ref[pl.ds(own0, BS), :].astype(BF16)
    qm = jnp.concatenate([qbf, mask_ref[pl.ds(own0, BS), :]], axis=1)
    half = BS // 2

    def fold(t):
        return jnp.maximum(t[:, 0:half], t[:, half:BS])

    qi = lax.broadcasted_iota(jnp.int32, (BS, BS), 0)
    ki = lax.broadcasted_iota(jnp.int32, (BS, BS), 1)
    s = _mm_nt(qbf, k_ref[pl.ds(own0, BS), :]) * scale + tiles_ref[0]
    s = jnp.where(ki <= qi, s, -jnp.inf)
    s_ref[qb] = s
    mx_ref[...] = fold(s)
    blk_lane = lax.broadcasted_iota(jnp.int32, (BS, LANES), 1)

    def logits(js):
        raw = []
        for j in js:
            kj = k_ref[pl.ds(pl.multiple_of(j * BS, BS), BS), :].astype(BF16)
            onehot = jnp.where(blk_lane == j, 1.0, 0.0).astype(BF16)
            raw.append(_mm_nt(qm, jnp.concatenate([kj, onehot], axis=1)))
        mx = mx_ref[...]
        for j, r in zip(js, raw):
            s = r * scale + tiles_ref[qb - j]
            s_ref[j] = s
            mx = jnp.maximum(mx, fold(s))
        mx_ref[...] = mx

    def unrolled(n, fn):
        def body(jj, _):
            fn([MOBA_UNROLL * jj + a for a in range(MOBA_UNROLL)])
            return 0

        lax.fori_loop(0, n // MOBA_UNROLL, body, 0)
        base = (n // MOBA_UNROLL) * MOBA_UNROLL
        step = MOBA_UNROLL // 2
        while step >= 1:
            @pl.when((n & step) != 0)
            def _(base=base, step=step):
                fn([base + a for a in range(step)])

            base = base + (n & step)
            step //= 2

    unrolled(qb, logits)
    m = jnp.max(mx_ref[...], axis=1, keepdims=True)

    ls_ref[...] = jnp.zeros_like(ls_ref)
    acc_ref[...] = jnp.zeros_like(acc_ref)

    def accumulate(js):
        ps = [jnp.exp(s_ref[j] - m) for j in js]
        pv = [_mm(p, v_ref[pl.ds(pl.multiple_of(j * BS, BS), BS), :]) for j, p in zip(js, ps)]
        ls, acc = ls_ref[...], acc_ref[...]
        for p, o in zip(ps, pv):
            ls = ls + (p[:, 0:half] + p[:, half:BS])
            acc = acc + o
        ls_ref[...] = ls
        acc_ref[...] = acc

    unrolled(qb + 1, accumulate)
    o_ref[...] = acc_ref[...] / jnp.sum(ls_ref[...], axis=1, keepdims=True)


def _moba_mixer(x, g, w_in, rel_table, w_out, bsz):
    T, D = x.shape
    L = T // bsz
    H, dh, BS = MOBA_HEADS, LANES, MOBA_BLOCK
    nb = L // BS
    proj = _rms_matmul(x, g, w_in, 512).reshape(bsz, L, 3 * D)
    rev_len = L + BS
    dist = np.maximum(L - 1 - np.arange(rev_len), 0)
    bucket_rev = jnp.asarray(_t5_bucket_np(dist)[None, :])
    bias_rev = pl.pallas_call(
        _bias_vec_kernel,
        out_shape=jax.ShapeDtypeStruct((H, rev_len), F32),
        name="moba_bias",
    )(bucket_rev, rel_table.T).reshape(H, 1, rev_len)
    o = pl.pallas_call(
        functools.partial(_moba_kernel, L=L),
        out_shape=jax.ShapeDtypeStruct((bsz, L, D), F32),
        grid=(H, bsz, nb),
        in_specs=[pl.BlockSpec((None, L, dh), lambda h, b, n: (b, 0, h)),
                  pl.BlockSpec((None, L, dh), lambda h, b, n: (b, 0, H + h)),
                  pl.BlockSpec((None, L, dh), lambda h, b, n: (b, 0, 2 * H + h)),
                  pl.BlockSpec((None, 1, rev_len), lambda h, b, n: (h, 0, 0))],
        out_specs=pl.BlockSpec((None, BS, dh), lambda h, b, n: (b, n, h)),
        scratch_shapes=[pltpu.VMEM((LANES, dh), F32),
                        pltpu.VMEM((L, LANES), BF16),
                        pltpu.VMEM((nb, BS, BS), F32),
                        pltpu.VMEM((nb, BS, BS), F32),
                        pltpu.VMEM((BS, BS // 2), F32),
                        pltpu.VMEM((BS, BS // 2), F32),
                        pltpu.VMEM((BS, dh), F32)],
        compiler_params=_cparams(("parallel", "arbitrary", "arbitrary")),
        name="moba",
    )(proj, proj, proj, bias_rev)
    return _matmul_residual(o.reshape(T, D), w_out, x)


def kernel(x, norm_g, final_norm_g, ffn_w_gate, ffn_w_up, ffn_w_down, s5_lam_re, s5_lam_im, s5_log_dt, s5_b_re, s5_b_im, s5_c_re, s5_c_im, s5_d, s5_w_glu, hg_w_in, hg_lower_bounds, hg_norm_g, hg_w_out, gdn_w_in, gdn_conv_w, gdn_a_log, gdn_dt_bias, gdn_norm_g, gdn_w_out, moba_w_in, moba_w_out, rel_bias_table):
    bsz, L, D = x.shape
    depth = norm_g.shape[0]
    h = x.reshape(bsz * L, D)
    for i in range(depth):
        m, j = i % 4, i // 4
        h = _ffn(h, norm_g[i, 0], ffn_w_gate[i, 0], ffn_w_up[i, 0], ffn_w_down[i, 0])
        if m == 0:
            h = _s5_mixer(h, norm_g[i, 1], s5_lam_re[j], s5_lam_im[j], s5_log_dt[j], s5_b_re[j],
                          s5_b_im[j], s5_c_re[j], s5_c_im[j], s5_d[j], s5_w_glu[j], bsz)
        elif m == 1:
            h = _hgrn_mixer(h, norm_g[i, 1], hg_w_in[j], hg_lower_bounds, hg_norm_g[j], hg_w_out[j],
                            bsz, i)
        elif m == 2:
            h = _gdn_mixer(h, norm_g[i, 1], gdn_w_in[j], gdn_conv_w[j], gdn_a_log[j], gdn_dt_bias[j],
                           gdn_norm_g[j], gdn_w_out[j], bsz)
        else:
            h = _moba_mixer(h, norm_g[i, 1], moba_w_in[j], rel_bias_table, moba_w_out[j], bsz)
        h = _ffn(h, norm_g[i, 2], ffn_w_gate[i, 1], ffn_w_up[i, 1], ffn_w_down[i, 1])
    return _rms(h, final_norm_g).reshape(bsz, L, D)
```

```python
import functools
import math

import jax
import jax.numpy as jnp
import numpy as np
from jax import lax
from jax.experimental import pallas as pl
from jax.experimental.pallas import tpu as pltpu

F32 = jnp.float32
BF16 = jnp.bfloat16
LANES = 128
RMS_EPS = 1e-6

HG_HEADS = 8
HG_CHUNK = 32
GDN_QK_HEADS = 8
GDN_V_HEADS = 16
GDN_CHUNK = 64
GDN_CONV = 4
MOBA_HEADS = 8
MOBA_BLOCK = 256
MOBA_TOPK = 3
REL_BUCKETS = 32
REL_MAX_DIST = 2048
S5_GROUP = 16
S5_STATE = 64
S5_TC = 16
S5_GP = LANES // S5_GROUP

VMEM_LIMIT = 56 * 1024 * 1024


def _cparams(sem):
    return pltpu.CompilerParams(dimension_semantics=sem, vmem_limit_bytes=VMEM_LIMIT)


def _mm(a, b):
    return jnp.dot(a.astype(BF16), b.astype(BF16), preferred_element_type=F32)


def _mm_nt(a, b):
    return lax.dot_general(a.astype(BF16), b.astype(BF16), (((1,), (1,)), ((), ())),
                           preferred_element_type=F32)


def _mm_tn(a, b):
    return lax.dot_general(a.astype(BF16), b.astype(BF16), (((0,), (0,)), ((), ())),
                           preferred_element_type=F32)


def _mm_hi(a, b):
    return jnp.dot(a, b, precision=lax.Precision.HIGHEST, preferred_element_type=F32)


def _mm_nt_hi(a, b):
    return lax.dot_general(a, b, (((1,), (1,)), ((), ())), precision=lax.Precision.HIGHEST,
                           preferred_element_type=F32)


def _sigmoid(x):
    return 1.0 / (1.0 + jnp.exp(-x))


def _silu(x):
    return x * _sigmoid(x)


def _rms_rows(x, g):
    return x * lax.rsqrt(jnp.mean(x * x, axis=-1, keepdims=True) + RMS_EPS) * g


FFN_TM = 512
FFN_TF = 256


def _ffn_kernel(x_ref, g_ref, wg_ref, wu_ref, wd_ref, *rest, final_norm):
    o_ref = rest[-1]
    x = x_ref[...]
    h = _rms_rows(x, g_ref[...]).astype(BF16)
    acc = None
    for c in range(wg_ref.shape[1] // FFN_TF):
        cols = slice(c * FFN_TF, (c + 1) * FFN_TF)
        gate = jnp.dot(h, wg_ref[:, cols], preferred_element_type=F32)
        up = jnp.dot(h, wu_ref[:, cols], preferred_element_type=F32)
        a = (_silu(gate) * up).astype(BF16)
        part = jnp.dot(a, wd_ref[cols, :], preferred_element_type=F32)
        acc = part if acc is None else acc + part
    y = x + 0.5 * acc
    o_ref[...] = _rms_rows(y, rest[0][...]) if final_norm else y


def _ffn(x, g, wg, wu, wd, layer, which, final_g=None):
    T, D = x.shape
    F = wg.shape[-1]
    tm = min(FFN_TM, T)
    row = pl.BlockSpec((1, D), lambda i: (0, 0))
    resident = pl.Buffered(1)
    in_specs = [pl.BlockSpec((tm, D), lambda i: (i, 0)), row,
                pl.BlockSpec((None, None, D, F), lambda i: (layer, which, 0, 0), pipeline_mode=resident),
                pl.BlockSpec((None, None, D, F), lambda i: (layer, which, 0, 0), pipeline_mode=resident),
                pl.BlockSpec((None, None, F, D), lambda i: (layer, which, 0, 0), pipeline_mode=resident)]
    args = [x, g.reshape(1, D), wg, wu, wd]
    if final_g is not None:
        in_specs.append(row)
        args.append(final_g.reshape(1, D))
    return pl.pallas_call(
        functools.partial(_ffn_kernel, final_norm=final_g is not None),
        out_shape=jax.ShapeDtypeStruct((T, D), F32),
        grid=(T // tm,),
        in_specs=in_specs,
        out_specs=pl.BlockSpec((tm, D), lambda i: (i, 0)),
        compiler_params=_cparams(("parallel",)),
        name="ffn",
    )(*args)


PROJ_TM = 512


def _rmsmm_kernel(x_ref, g_ref, w_ref, o_ref, *, tn):
    h = _rms_rows(x_ref[...], g_ref[...]).astype(BF16)
    for c in range(w_ref.shape[1] // tn):
        cols = slice(c * tn, (c + 1) * tn)
        o_ref[:, cols] = jnp.dot(h, w_ref[:, cols], preferred_element_type=F32)


def _rms_matmul(x, g, w, tn):
    T, D = x.shape
    N = w.shape[1]
    tm = min(PROJ_TM, T)
    return pl.pallas_call(
        functools.partial(_rmsmm_kernel, tn=tn),
        out_shape=jax.ShapeDtypeStruct((T, N), F32),
        grid=(T // tm,),
        in_specs=[pl.BlockSpec((tm, D), lambda i: (i, 0)),
                  pl.BlockSpec((1, D), lambda i: (0, 0)),
                  pl.BlockSpec((D, N), lambda i: (0, 0), pipeline_mode=pl.Buffered(1))],
        out_specs=pl.BlockSpec((tm, N), lambda i: (i, 0)),
        compiler_params=_cparams(("parallel",)),
        name="rms_matmul",
    )(x, g.reshape(1, D), w.astype(BF16))


def _mmres_kernel(a_ref, w_ref, x_ref, o_ref):
    o_ref[...] = x_ref[...] + jnp.dot(a_ref[...].astype(BF16), w_ref[...],
                                      preferred_element_type=F32)


def _matmul_residual(a, w, x):
    T, K = a.shape
    N = w.shape[1]
    tm = min(PROJ_TM, T)
    return pl.pallas_call(
        _mmres_kernel,
        out_shape=jax.ShapeDtypeStruct((T, N), F32),
        grid=(T // tm,),
        in_specs=[pl.BlockSpec((tm, K), lambda i: (i, 0)),
                  pl.BlockSpec((K, N), lambda i: (0, 0), pipeline_mode=pl.Buffered(1)),
                  pl.BlockSpec((tm, N), lambda i: (i, 0))],
        out_specs=pl.BlockSpec((tm, N), lambda i: (i, 0)),
        compiler_params=_cparams(("parallel",)),
        name="matmul_residual",
    )(a, w.astype(BF16), x)


def _cmul(ar, ai, br, bi):
    return ar * br - ai * bi, ar * bi + ai * br


def _s5_powers(lr, li, ldt, n):
    lr = jnp.minimum(lr, -1e-4)
    dt = jnp.exp(ldt)
    mag = jnp.exp(lr * dt)
    br = mag * jnp.cos(li * dt)
    bi = mag * jnp.sin(li * dt)
    den = lr * lr + li * li
    fr = ((br - 1.0) * lr + bi * li) / den
    fi = (bi * lr - (br - 1.0) * li) / den
    pw = [(jnp.ones_like(br), jnp.zeros_like(bi))]
    for _ in range(n):
        pw.append(_cmul(pw[-1][0], pw[-1][1], br, bi))
    return pw, (fr, fi)


def _s5_prep_kernel(lr_row, li_row, ldt_row, lr_col, li_col, ldt_col, btr_ref, bti_ref, ctr_ref,
                    cti_ref, bd_ref, bexp_ref, cexp_ref, a_ref):
    tc = S5_TC
    ns = lr_row.shape[-1]
    pw_row, (fr, fi) = _s5_powers(lr_row[...], li_row[...], ldt_row[...], tc)
    pw_col, _ = _s5_powers(lr_col[...], li_col[...], ldt_col[...], tc)
    btr, bti = btr_ref[...], bti_ref[...]
    ctr, cti = ctr_ref[...], cti_ref[...]
    for k in range(tc):
        fer, fei = _cmul(fr, fi, pw_row[k][0], pw_row[k][1])
        xr = btr * fer - bti * fei
        xi = btr * fei + bti * fer
        bd_ref[k] = (_mm_hi(xr, ctr) - _mm_hi(xi, cti)).astype(BF16)
        s = tc - 1 - k
        bexp_ref[s * LANES:(s + 1) * LANES, 0:ns] = xr.astype(BF16)
        bexp_ref[s * LANES:(s + 1) * LANES, ns:2 * ns] = xi.astype(BF16)
        er, ei = pw_col[k + 1]
        cexp_ref[0:ns, k * LANES:(k + 1) * LANES] = (ctr * er - cti * ei).astype(BF16)
        cexp_ref[ns:2 * ns, k * LANES:(k + 1) * LANES] = (-(ctr * ei + cti * er)).astype(BF16)
    a_ref[:, 0:ns] = pw_row[tc][0]
    a_ref[:, ns:2 * ns] = pw_row[tc][1]


def _s5_main_kernel(u_ref, bd_ref, bexp_ref, cexp_ref, a_ref, y_ref, m_ref, hl_ref, hp_ref):
    tc = S5_TC
    ns = a_ref.shape[-1] // 2
    rows = u_ref.shape[0]

    @pl.when(pl.program_id(1) == 0)
    def _():
        for s in range(tc):
            for t in range(tc):
                blk = bd_ref[t - s] if t >= s else jnp.zeros((LANES, LANES), BF16)
                m_ref[s * LANES:(s + 1) * LANES, t * LANES:(t + 1) * LANES] = blk

    u = u_ref[...].astype(BF16)
    hl_ref[...] = jnp.dot(u, bexp_ref[...], preferred_element_type=F32)
    are, aim = a_ref[:, 0:ns], a_ref[:, ns:2 * ns]

    def step(c, carry):
        hre, him = carry
        hp_ref[pl.ds(c, 1), 0:ns] = hre
        hp_ref[pl.ds(c, 1), ns:2 * ns] = him
        xre = hl_ref[pl.ds(c, 1), 0:ns]
        xim = hl_ref[pl.ds(c, 1), ns:2 * ns]
        return are * hre - aim * him + xre, are * him + aim * hre + xim

    z = jnp.zeros((1, ns), F32)
    lax.fori_loop(0, rows, step, (z, z))
    y_ref[...] = (jnp.dot(u, m_ref[...], preferred_element_type=F32)
                  + jnp.dot(hp_ref[...].astype(BF16), cexp_ref[...], preferred_element_type=F32))


def _s5_pre_kernel(x_ref, g_ref, ug_ref, u_ref):
    tc = S5_TC
    ng, r, _ = ug_ref.shape
    u = _rms_rows(x_ref[...], g_ref[...])
    for n in range(ng):
        u_ref[n] = u[:, n * LANES:(n + 1) * LANES]
    for n in range(ng):
        for t in range(tc):
            ug_ref[n, :, t * LANES:(t + 1) * LANES] = (
                u_ref[n, pl.ds(t, r, stride=tc), :].astype(BF16))


def _s5_post_kernel(yg_ref, x_ref, g_ref, d_ref, wv_ref, wg_ref, o_ref, a_ref, y_ref):
    tc = S5_TC
    ng, r, _ = yg_ref.shape
    tn = o_ref.shape[1]
    j = pl.program_id(1)

    @pl.when(j == 0)
    def _():
        for n in range(ng):
            for t in range(tc):
                y_ref[n, pl.ds(t, r, stride=tc), :] = yg_ref[n, :, t * LANES:(t + 1) * LANES]
        du = d_ref[...] * _rms_rows(x_ref[...], g_ref[...])
        c = math.sqrt(2.0 / math.pi)
        for n in range(ng):
            y = y_ref[n] + du[:, n * LANES:(n + 1) * LANES]
            a_ref[:, n * LANES:(n + 1) * LANES] = (
                0.5 * y * (1.0 + jnp.tanh(c * (y + 0.044715 * (y * y * y))))).astype(BF16)

    a = a_ref[...]
    val = jnp.dot(a, wv_ref[...].astype(BF16), preferred_element_type=F32)
    gate = jnp.dot(a, wg_ref[...].astype(BF16), preferred_element_type=F32)
    o_ref[...] = x_ref[:, pl.ds(pl.multiple_of(j * tn, tn), tn)] + val * _sigmoid(gate)


def _s5_mixer(x, g, lam_re, lam_im, log_dt, b_re, b_im, c_re, c_im, d_skip, w_glu, bsz):
    T, D = x.shape
    L = T // bsz
    G, P = lam_re.shape
    tc, gp = S5_TC, S5_GP
    ng = G // gp
    ns = gp * P
    rows = L // tc

    eye = jnp.eye(gp, dtype=F32)

    def bt(b):
        return jnp.einsum('ngph,gk->nghkp', b.reshape(ng, gp, P, S5_GROUP), eye).reshape(ng, LANES, ns)

    def ct(c):
        return jnp.einsum('nghp,gk->ngpkh', c.reshape(ng, gp, S5_GROUP, P), eye).reshape(ng, ns, LANES)

    lr, li = lam_re.reshape(ng, 1, ns), lam_im.reshape(ng, 1, ns)
    ldt = jnp.broadcast_to(log_dt[:, None], (G, P)).reshape(ng, 1, ns)
    row_spec = pl.BlockSpec((None, 1, ns), lambda n: (n, 0, 0))
    col_spec = pl.BlockSpec((None, ns, 1), lambda n: (n, 0, 0))
    bd, bexp, cexp, a16 = pl.pallas_call(
        _s5_prep_kernel,
        out_shape=(jax.ShapeDtypeStruct((ng, tc, LANES, LANES), BF16),
                   jax.ShapeDtypeStruct((ng, tc * LANES, 2 * ns), BF16),
                   jax.ShapeDtypeStruct((ng, 2 * ns, tc * LANES), BF16),
                   jax.ShapeDtypeStruct((ng, 1, 2 * ns), F32)),
        grid=(ng,),
        in_specs=[row_spec, row_spec, row_spec, col_spec, col_spec, col_spec,
                  pl.BlockSpec((None, LANES, ns), lambda n: (n, 0, 0)),
                  pl.BlockSpec((None, LANES, ns), lambda n: (n, 0, 0)),
                  pl.BlockSpec((None, ns, LANES), lambda n: (n, 0, 0)),
                  pl.BlockSpec((None, ns, LANES), lambda n: (n, 0, 0))],
        out_specs=(pl.BlockSpec((None, tc, LANES, LANES), lambda n: (n, 0, 0, 0)),
                   pl.BlockSpec((None, tc * LANES, 2 * ns), lambda n: (n, 0, 0)),
                   pl.BlockSpec((None, 2 * ns, tc * LANES), lambda n: (n, 0, 0)),
                   pl.BlockSpec((None, 1, 2 * ns), lambda n: (n, 0, 0))),
        compiler_params=_cparams(("parallel",)),
        name="s5_prep",
    )(lr, li, ldt, lr.reshape(ng, ns, 1), li.reshape(ng, ns, 1), ldt.reshape(ng, ns, 1),
      bt(b_re), bt(b_im), ct(c_re), ct(c_im))

    tm = min(1024, T)
    ug = pl.pallas_call(
        _s5_pre_kernel,
        out_shape=jax.ShapeDtypeStruct((ng, T // tc, tc * LANES), BF16),
        grid=(T // tm,),
        in_specs=[pl.BlockSpec((tm, D), lambda i: (i, 0)), pl.BlockSpec((1, D), lambda i: (0, 0))],
        out_specs=pl.BlockSpec((ng, tm // tc, tc * LANES), lambda i: (0, i, 0)),
        scratch_shapes=[pltpu.VMEM((ng, tm, LANES), F32)],
        compiler_params=_cparams(("parallel",)),
        name="s5_pre",
    )(x, g.reshape(1, D))
    yg = pl.pallas_call(
        _s5_main_kernel,
        out_shape=jax.ShapeDtypeStruct((ng, T // tc, tc * LANES), F32),
        grid=(ng, bsz),
        in_specs=[pl.BlockSpec((None, rows, tc * LANES), lambda n, b: (n, b, 0)),
                  pl.BlockSpec((None, tc, LANES, LANES), lambda n, b: (n, 0, 0, 0)),
                  pl.BlockSpec((None, tc * LANES, 2 * ns), lambda n, b: (n, 0, 0)),
                  pl.BlockSpec((None, 2 * ns, tc * LANES), lambda n, b: (n, 0, 0)),
                  pl.BlockSpec((None, 1, 2 * ns), lambda n, b: (n, 0, 0))],
        out_specs=pl.BlockSpec((None, rows, tc * LANES), lambda n, b: (n, b, 0)),
        scratch_shapes=[pltpu.VMEM((tc * LANES, tc * LANES), BF16),
                        pltpu.VMEM((rows, 2 * ns), F32),
                        pltpu.VMEM((rows, 2 * ns), F32)],
        compiler_params=_cparams(("parallel", "arbitrary")),
        name="s5_main",
    )(ug, bd, bexp, cexp, a16)

    tn = 512
    nj = D // tn
    return pl.pallas_call(
        _s5_post_kernel,
        out_shape=jax.ShapeDtypeStruct((T, D), F32),
        grid=(T // tm, nj),
        in_specs=[pl.BlockSpec((ng, tm // tc, tc * LANES), lambda i, j: (0, i, 0)),
                  pl.BlockSpec((tm, D), lambda i, j: (i, 0)),
                  pl.BlockSpec((1, D), lambda i, j: (0, 0)),
                  pl.BlockSpec((1, D), lambda i, j: (0, 0)),
                  pl.BlockSpec((D, tn), lambda i, j: (0, j)),
                  pl.BlockSpec((D, tn), lambda i, j: (0, j + nj))],
        out_specs=pl.BlockSpec((tm, tn), lambda i, j: (i, j)),
        scratch_shapes=[pltpu.VMEM((tm, D), BF16), pltpu.VMEM((ng, tm, LANES), F32)],
        compiler_params=_cparams(("parallel", "arbitrary")),
        name="s5_post",
    )(yg, x, g.reshape(1, D), d_skip.reshape(1, D), w_glu, w_glu)


HG_SUPER = 256


def _hgrn_kernel(q_ref, f_ref, i_ref, g_ref, lb_ref, ng_ref, o_ref, st_ref, *, layer_idx):
    L, dk = q_ref.shape
    C, SC = HG_CHUNK, HG_SUPER
    lbraw = lb_ref[...]
    e = jnp.exp(lbraw - jnp.max(lbraw, axis=0, keepdims=True))
    sm = e / jnp.sum(e, axis=0, keepdims=True)
    layer = lax.broadcasted_iota(jnp.int32, sm.shape, 0)
    lb = jnp.sum(jnp.where((layer >= 1) & (layer <= layer_idx), sm, 0.0), axis=0, keepdims=True)

    ti = lax.broadcasted_iota(jnp.int32, (SC, SC), 0)
    si = lax.broadcasted_iota(jnp.int32, (SC, SC), 1)
    causal = ((ti // C) == (si // C)) & (si <= ti)
    pos = lax.broadcasted_iota(jnp.int32, (SC, dk), 0) & (C - 1)
    ng = ng_ref[...]
    st_ref[...] = jnp.zeros_like(st_ref)

    def body(sc, _):
        r0 = pl.multiple_of(sc * SC, SC)
        f = f_ref[pl.ds(r0, SC), :]
        fgate = lb + (1.0 - lb) * _sigmoid(f)
        logf = jnp.log(fgate)
        k = (1.0 - lb) * _sigmoid(-f)
        q = _silu(q_ref[pl.ds(r0, SC), :]) * dk ** -0.5
        v = i_ref[pl.ds(r0, SC), :]
        b, rest = logf, logf
        shift = 1
        while shift < C:
            b = b + jnp.where(pos >= shift, pltpu.roll(b, shift, 0), 0.0)
            rest = rest + jnp.where(pos + shift < C, pltpu.roll(rest, SC - shift, 0), 0.0)
            shift *= 2
        q_t = q * jnp.exp(b)
        k_t = k * jnp.exp(-b)
        k_state = k * jnp.exp(rest - logf)
        attn = jnp.where(causal, _mm_nt(q_t, k_t), 0.0)
        o = _mm(attn, v)
        nc = SC // C
        outer = [_mm_tn(v[c * C:(c + 1) * C], k_state[c * C:(c + 1) * C]) for c in range(nc)]
        states = [st_ref[...]]
        for c in range(nc):
            dc = jnp.exp(b[(c + 1) * C - 1:(c + 1) * C, :])
            states.append(states[c] * dc + outer[c])
        st_ref[...] = states[nc]
        inter = [_mm_nt(q_t[c * C:(c + 1) * C], states[c]) for c in range(nc)]
        o = o + jnp.concatenate(inter, axis=0)
        o = _rms_rows(o, ng) * _silu(g_ref[pl.ds(r0, SC), :])
        o_ref[pl.ds(r0, SC), :] = o
        return 0

    lax.fori_loop(0, L // SC, body, 0)


def _hgrn_mixer(x, g, w_in, lower_bounds, norm_g, w_out, bsz, layer_idx):
    T, D = x.shape
    L = T // bsz
    H = HG_HEADS
    dk = D // H
    proj = _rms_matmul(x, g, w_in, 512).reshape(bsz, L, 4 * D)
    nl = lower_bounds.shape[0]

    def col(off):
        return pl.BlockSpec((None, L, dk), lambda b, h: (b, 0, off * H + h))

    o = pl.pallas_call(
        functools.partial(_hgrn_kernel, layer_idx=layer_idx),
        out_shape=jax.ShapeDtypeStruct((bsz, L, D), F32),
        grid=(bsz, H),
        in_specs=[col(0), col(1), col(2), col(3),
                  pl.BlockSpec((nl, dk), lambda b, h: (0, h)),
                  pl.BlockSpec((1, dk), lambda b, h: (0, 0))],
        out_specs=pl.BlockSpec((None, L, dk), lambda b, h: (b, 0, h)),
        scratch_shapes=[pltpu.VMEM((dk, dk), F32)],
        compiler_params=_cparams(("parallel", "parallel")),
        name="hgrn",
    )(proj, proj, proj, proj, lower_bounds, norm_g.reshape(1, dk))
    return _matmul_residual(o.reshape(T, D), w_out, x)


GDN_LTILE = 512
GDN_QK_PER_STEP = 4
GDN_SOLVE_CHUNKS = 2
GDN_SOLVE_WIDE_STEPS = 1


def _split3(x):
    x1 = x.astype(BF16)
    r = x - x1.astype(F32)
    x2 = r.astype(BF16)
    return x1, x2, (r - x2.astype(F32)).astype(BF16)


def _mm_sel(x, sel):
    s = sel.astype(BF16)
    p1, p2, p3 = (jnp.dot(p, s, preferred_element_type=F32) for p in _split3(x))
    return p1 + (p2 + p3)


def _mm_sel_nt(sel, x):
    s = sel.astype(BF16)
    p1, p2, p3 = (lax.dot_general(s, p, (((1,), (1,)), ((), ())), preferred_element_type=F32)
                  for p in _split3(x))
    return p1 + (p2 + p3)


def _mm3(a, b):
    a1 = a.astype(BF16)
    a2 = (a - a1.astype(F32)).astype(BF16)
    b1 = b.astype(BF16)
    b2 = (b - b1.astype(F32)).astype(BF16)
    return (jnp.dot(a1, b1, preferred_element_type=F32)
            + (jnp.dot(a1, b2, preferred_element_type=F32) + jnp.dot(a2, b1, preferred_element_type=F32)))


def _gdn_kernel(q_ref, k_ref, v_ref, z_ref, gc_ref, cwq_ref, cwk_ref, cwv_ref, alog_ref, dtb_ref,
                ng_ref, o_ref, xq_ref, xk_ref, xv_ref, qn_ref, kn_ref, vc_ref, bcol_ref, gcol_ref,
                beta_ref, bcum_ref, u_ref, w_ref, qk_ref, s_ref):
    Lt = q_ref.shape[0]
    dk = LANES
    C = GDN_CHUNK
    nqk = GDN_QK_PER_STEP
    nh = nqk * (GDN_V_HEADS // GDN_QK_HEADS)
    hg = pl.program_id(1)
    first = pl.program_id(2) == 0
    pad = 8

    @pl.when(first)
    def _():
        s_ref[...] = jnp.zeros_like(s_ref)
        xq_ref[0:pad, :] = jnp.zeros((pad, xq_ref.shape[1]), F32)
        xk_ref[0:pad, :] = jnp.zeros((pad, xk_ref.shape[1]), F32)
        xv_ref[0:pad, :] = jnp.zeros((pad, xv_ref.shape[1]), F32)

    def conv_silu(src_ref, w_ref, xp_ref):
        xp_ref[pad:pad + Lt, :] = src_ref[...]
        w = w_ref[...]
        acc = None
        for j in range(GDN_CONV):
            s = pad - (GDN_CONV - 1) + j
            term = w[j:j + 1, :] * xp_ref[s:s + Lt, :]
            acc = term if acc is None else acc + term
        xp_ref[0:pad, :] = xp_ref[Lt:Lt + pad, :]
        return _silu(acc)

    def l2n(t):
        return t * lax.rsqrt(jnp.sum(t * t, axis=-1, keepdims=True) + 1e-6)

    qa = conv_silu(q_ref, cwq_ref, xq_ref)
    ka = conv_silu(k_ref, cwk_ref, xk_ref)
    for i in range(nqk):
        qn_ref[:, i * dk:(i + 1) * dk] = l2n(qa[:, i * dk:(i + 1) * dk]) * dk ** -0.5
        kn_ref[:, i * dk:(i + 1) * dk] = l2n(ka[:, i * dk:(i + 1) * dk])
    vc_ref[...] = conv_silu(v_ref, cwv_ref, xv_ref)

    gcraw = gc_ref[...]
    bcol_ref[...] = _sigmoid(gcraw)
    xs = gcraw + dtb_ref[...]
    softplus = jnp.maximum(xs, 0.0) + jnp.log(1.0 + jnp.exp(-jnp.abs(xs)))
    gcol_ref[...] = -jnp.exp(alog_ref[...]) * softplus

    row = lax.broadcasted_iota(jnp.int32, (Lt, LANES), 0) & (C - 1)
    bc_all = gcol_ref[...]
    shift = 1
    while shift < C:
        bc_all = bc_all + jnp.where(row >= shift, pltpu.roll(bc_all, shift, 0), 0.0)
        shift *= 2
    sel_r = lax.broadcasted_iota(jnp.int32, (LANES, LANES), 0)
    braw = bcol_ref[...]
    for j in range(nh):
        hv = nh * hg + j
        beta_ref[j] = _mm_sel(braw, jnp.where(sel_r == hv, 1.0, 0.0))
        bcum_ref[j] = _mm_sel(bc_all, jnp.where(sel_r == hv + GDN_V_HEADS, 1.0, 0.0))

    ti = lax.broadcasted_iota(jnp.int32, (C, C), 0)
    si = lax.broadcasted_iota(jnp.int32, (C, C), 1)
    incl = si <= ti
    strict = si < ti
    eye_c = jnp.where(si == ti, 1.0, 0.0)
    ng = ng_ref[...]
    rep = nh // nqk

    def solve_body(cc, _):
        rows = [pl.ds(pl.multiple_of((cc * GDN_SOLVE_CHUNKS + a) * C, C), C)
                for a in range(GDN_SOLVE_CHUNKS)]
        kk, qk0 = {}, {}
        for a, rw in enumerate(rows):
            for i in range(nqk):
                kc = kn_ref[rw, i * dk:(i + 1) * dk]
                kk[a, i] = _mm_nt(kc, kc)
                qk0[a, i] = _mm_nt(qn_ref[rw, i * dk:(i + 1) * dk], kc)
        probs = [(a, j) for a in range(GDN_SOLVE_CHUNKS) for j in range(nh)]
        pmat, sol, decay = {}, {}, {}
        for a, j in probs:
            i = j // rep
            bcum = bcum_ref[j, rows[a], :]
            beta = beta_ref[j, rows[a], :]
            dmat = bcum[:, 0:C] - jnp.transpose(bcum)[0:1, :]
            decay[a, j] = jnp.where(incl, jnp.exp(jnp.where(incl, dmat, 0.0)), 0.0)
            pmat[a, j] = jnp.where(strict, -(kk[a, i] * beta[:, 0:C]) * decay[a, j], 0.0)
            sol[a, j] = jnp.concatenate(
                [vc_ref[rows[a], j * dk:(j + 1) * dk] * beta,
                 kn_ref[rows[a], i * dk:(i + 1) * dk] * beta * jnp.exp(bcum)], axis=1)
        inv = {p: eye_c + pmat[p] for p in probs}
        pmat = {p: _mm3(pmat[p], pmat[p]) for p in probs}
        for it in range(1, 6):
            mm = _mm3 if it < GDN_SOLVE_WIDE_STEPS else _mm
            if it < 5:
                prod = {p: mm(pmat[p], jnp.concatenate([inv[p], pmat[p]], axis=1)) for p in probs}
                inv = {p: inv[p] + prod[p][:, 0:C] for p in probs}
                pmat = {p: prod[p][:, C:2 * C] for p in probs}
            else:
                inv = {p: inv[p] + mm(pmat[p], inv[p]) for p in probs}
        sol = {p: _mm3(inv[p], sol[p]) for p in probs}
        for a, j in probs:
            u_ref[j, rows[a], :] = sol[a, j][:, 0:dk]
            w_ref[j, rows[a], :] = sol[a, j][:, dk:2 * dk]
            qk_ref[j, rows[a], :] = qk0[a, j // rep] * decay[a, j]
        return 0

    lax.fori_loop(0, Lt // (C * GDN_SOLVE_CHUNKS), solve_body, 0)

    def state_body(c, _):
        r0 = pl.multiple_of(c * C, C)
        rows = pl.ds(r0, C)
        heads = range(nh)
        S, blast, k_state, ws = [], [], [], []
        for j in heads:
            i = j // rep
            bcum = bcum_ref[j, rows, :]
            blast.append(bcum_ref[j, pl.ds(r0 + C - 1, 1), :])
            q_dec = qn_ref[rows, i * dk:(i + 1) * dk] * jnp.exp(bcum)
            k_state.append(kn_ref[rows, i * dk:(i + 1) * dk] * jnp.exp(blast[j] - bcum))
            S.append(s_ref[j])
            ws.append(_mm(jnp.concatenate([w_ref[j, rows, :], q_dec], axis=0), S[j]))
        v_new = [u_ref[j, rows, :] - ws[j][0:C] for j in heads]
        for j in heads:
            s_ref[j] = S[j] * jnp.exp(blast[j]) + _mm_tn(k_state[j], v_new[j])
        for j in heads:
            o = ws[j][C:2 * C] + _mm(qk_ref[j, rows, :], v_new[j])
            z = z_ref[rows, j * dk:(j + 1) * dk]
            o_ref[rows, j * dk:(j + 1) * dk] = _rms_rows(o, ng) * _silu(z)
        return 0

    lax.fori_loop(0, Lt // C, state_body, 0)


def _gdn_mixer(x, g, w_in, conv_w, a_log, dt_bias, norm_g, w_out, bsz):
    T, D = x.shape
    L = T // bsz
    Hk, Hv = GDN_QK_HEADS, GDN_V_HEADS
    dk = LANES
    nqk = GDN_QK_PER_STEP
    nh = nqk * (Hv // Hk)
    ngrp = Hk // nqk
    qw, vw = nqk * dk, nh * dk
    lt = min(GDN_LTILE, L)
    n_in = w_in.shape[1]
    n_pad = -(-n_in // (7 * LANES)) * (7 * LANES)
    w_pad = jnp.pad(w_in, ((0, 0), (0, n_pad - n_in)))
    proj = _rms_matmul(x, g, w_pad, 7 * LANES).reshape(bsz, L, n_pad)
    qkv_w = 2 * Hk * dk + Hv * dk
    gate_blk = (qkv_w + Hv * dk) // LANES
    alog_row = jnp.zeros((1, LANES), F32).at[0, Hv:2 * Hv].set(a_log)
    dtb_row = jnp.zeros((1, LANES), F32).at[0, Hv:2 * Hv].set(dt_bias)
    o = pl.pallas_call(
        _gdn_kernel,
        out_shape=jax.ShapeDtypeStruct((bsz, L, Hv * dk), F32),
        grid=(bsz, ngrp, L // lt),
        in_specs=[pl.BlockSpec((None, lt, qw), lambda b, h, t: (b, t, h)),
                  pl.BlockSpec((None, lt, qw), lambda b, h, t: (b, t, ngrp + h)),
                  pl.BlockSpec((None, lt, vw), lambda b, h, t: (b, t, ngrp + h)),
                  pl.BlockSpec((None, lt, vw), lambda b, h, t: (b, t, 2 * ngrp + h)),
                  pl.BlockSpec((None, lt, LANES), lambda b, h, t: (b, t, gate_blk)),
                  pl.BlockSpec((GDN_CONV, qw), lambda b, h, t: (0, h)),
                  pl.BlockSpec((GDN_CONV, qw), lambda b, h, t: (0, ngrp + h)),
                  pl.BlockSpec((GDN_CONV, vw), lambda b, h, t: (0, ngrp + h)),
                  pl.BlockSpec((1, LANES), lambda b, h, t: (0, 0)),
                  pl.BlockSpec((1, LANES), lambda b, h, t: (0, 0)),
                  pl.BlockSpec((1, dk), lambda b, h, t: (0, 0))],
        out_specs=pl.BlockSpec((None, lt, vw), lambda b, h, t: (b, t, h)),
        scratch_shapes=[pltpu.VMEM((lt + 8, qw), F32),
                        pltpu.VMEM((lt + 8, qw), F32),
                        pltpu.VMEM((lt + 8, vw), F32),
                        pltpu.VMEM((lt, qw), F32),
                        pltpu.VMEM((lt, qw), F32),
                        pltpu.VMEM((lt, vw), F32),
                        pltpu.VMEM((lt, LANES), F32),
                        pltpu.VMEM((lt, LANES), F32),
                        pltpu.VMEM((nh, lt, LANES), F32),
                        pltpu.VMEM((nh, lt, LANES), F32),
                        pltpu.VMEM((nh, lt, dk), F32),
                        pltpu.VMEM((nh, lt, dk), F32),
                        pltpu.VMEM((nh, lt, GDN_CHUNK), F32),
                        pltpu.VMEM((nh, dk, dk), F32)],
        compiler_params=_cparams(("parallel", "parallel", "arbitrary")),
        name="gdn",
    )(proj, proj, proj, proj, proj, conv_w, conv_w, conv_w, alog_row, dtb_row, norm_g.reshape(1, dk))
    return _matmul_residual(o.reshape(T, Hv * dk), w_out, x)


def _t5_bucket_np(n):
    max_exact = REL_BUCKETS // 2
    nf = np.maximum(n, 1).astype(np.float32)
    large = max_exact + (np.log(nf / np.float32(max_exact)) / np.float32(math.log(REL_MAX_DIST / max_exact))
                         * np.float32(REL_BUCKETS - max_exact)).astype(np.int32)
    large = np.minimum(large, REL_BUCKETS - 1)
    return np.where(n < max_exact, n, large).astype(np.int32)


def _bias_vec_kernel(bucket_ref, table_ref, o_ref):
    bucket = bucket_ref[...]
    table = table_ref[...]
    acc = jnp.zeros(o_ref.shape, F32)
    for k in range(REL_BUCKETS):
        acc = acc + jnp.where(bucket == k, table[:, k:k + 1], 0.0)
    o_ref[...] = acc


MOBA_MASK = 1e30
MOBA_UNROLL = 4


def _moba_kernel(q_ref, k_ref, v_ref, bias_ref, o_ref, kmean_ref, mask_ref, tiles_ref, s_ref, mx_ref,
                 ls_ref, acc_ref, *, L):
    BS = MOBA_BLOCK
    dh = q_ref.shape[-1]
    nb = L // BS
    qb = pl.program_id(2)
    rev_len = bias_ref.shape[-1]

    @pl.when((pl.program_id(1) == 0) & (qb == 0))
    def _():
        for d in range(nb):
            start = rev_len - 2 * BS - d * BS
            wb = jnp.broadcast_to(bias_ref[:, start:start + 2 * BS], (BS, 2 * BS))
            tiles_ref[d] = pltpu.roll(wb, BS + 1, 1, stride=1, stride_axis=0)[:, 0:BS]

    @pl.when(qb == 0)
    def _():
        kmean_ref[...] = jnp.zeros_like(kmean_ref)
        for n in range(nb):
            kmean_ref[n:n + 1, :] = jnp.sum(k_ref[n * BS:(n + 1) * BS, :], axis=0, keepdims=True) / BS
        blk_row = lax.broadcasted_iota(jnp.int32, (nb, L), 0)
        q_blk = lax.broadcasted_iota(jnp.int32, (nb, L), 1) // BS
        past = blk_row < q_blk
        gate = jnp.where(past, _mm_nt_hi(kmean_ref[0:nb, :], q_ref[...]), -1e30)
        sel_t = jnp.zeros((nb, L), F32)
        for _ in range(MOBA_TOPK):
            mx = jnp.max(gate, axis=0, keepdims=True)
            first = jnp.min(jnp.where(gate == mx, blk_row, nb), axis=0, keepdims=True)
            pick = blk_row == first
            sel_t = jnp.where(pick & past, 1.0, sel_t)
            gate = jnp.where(pick, -jnp.inf, gate)
        eye = jnp.where(lax.broadcasted_iota(jnp.int32, (nb, LANES), 0)
                        == lax.broadcasted_iota(jnp.int32, (nb, LANES), 1), 1.0, 0.0)
        mask_ref[...] = _mm_tn((sel_t - 1.0) * MOBA_MASK, eye).astype(BF16)

    scale = dh ** -0.5
    own0 = pl.multiple_of(qb * BS, BS)
    qbf = q_ref[pl.ds(own0, BS), :].astype(BF16)
    qm = jnp.concatenate([qbf, mask_ref[pl.ds(own0, BS), :]], axis=1)
    half = BS // 2

    def fold(t):
        return jnp.maximum(t[:, 0:half], t[:, half:BS])

    qi = lax.broadcasted_iota(jnp.int32, (BS, BS), 0)
    ki = lax.broadcasted_iota(jnp.int32, (BS, BS), 1)
    s = _mm_nt(qbf, k_ref[pl.ds(own0, BS), :]) * scale + tiles_ref[0]
    s = jnp.where(ki <= qi, s, -jnp.inf)
    s_ref[qb] = s
    mx_ref[...] = fold(s)
    blk_lane = lax.broadcasted_iota(jnp.int32, (BS, LANES), 1)

    def logits(js):
        raw = []
        for j in js:
            kj = k_ref[pl.ds(pl.multiple_of(j * BS, BS), BS), :].astype(BF16)
            onehot = jnp.where(blk_lane == j, 1.0, 0.0).astype(BF16)
            raw.append(_mm_nt(qm, jnp.concatenate([kj, onehot], axis=1)))
        mx = mx_ref[...]
        for j, r in zip(js, raw):
            s = r * scale + tiles_ref[qb - j]
            s_ref[j] = s
            mx = jnp.maximum(mx, fold(s))
        mx_ref[...] = mx

    def unrolled(n, fn):
        def body(jj, _):
            fn([MOBA_UNROLL * jj + a for a in range(MOBA_UNROLL)])
            return 0

        lax.fori_loop(0, n // MOBA_UNROLL, body, 0)
        base = (n // MOBA_UNROLL) * MOBA_UNROLL
        step = MOBA_UNROLL // 2
        while step >= 1:
            @pl.when((n & step) != 0)
            def _(base=base, step=step):
                fn([base + a for a in range(step)])

            base = base + (n & step)
            step //= 2

    unrolled(qb, logits)
    m = jnp.max(mx_ref[...], axis=1, keepdims=True)

    ls_ref[...] = jnp.zeros_like(ls_ref)
    acc_ref[...] = jnp.zeros_like(acc_ref)

    def accumulate(js):
        ps = [jnp.exp(s_ref[j] - m) for j in js]
        pv = [_mm(p, v_ref[pl.ds(pl.multiple_of(j * BS, BS), BS), :]) for j, p in zip(js, ps)]
        ls, acc = ls_ref[...], acc_ref[...]
        for p, o in zip(ps, pv):
            ls = ls + (p[:, 0:half] + p[:, half:BS])
            acc = acc + o
        ls_ref[...] = ls
        acc_ref[...] = acc

    unrolled(qb + 1, accumulate)
    o_ref[...] = acc_ref[...] / jnp.sum(ls_ref[...], axis=1, keepdims=True)


def _moba_mixer(x, g, w_in, rel_table, w_out, bsz):
    T, D = x.shape
    L = T // bsz
    H, dh, BS = MOBA_HEADS, LANES, MOBA_BLOCK
    nb = L // BS
    proj = _rms_matmul(x, g, w_in, 512).reshape(bsz, L, 3 * D)
    rev_len = L + BS
    dist = np.maximum(L - 1 - np.arange(rev_len), 0)
    bucket_rev = jnp.asarray(_t5_bucket_np(dist)[None, :])
    bias_rev = pl.pallas_call(
        _bias_vec_kernel,
        out_shape=jax.ShapeDtypeStruct((H, rev_len), F32),
        name="moba_bias",
    )(bucket_rev, rel_table.T).reshape(H, 1, rev_len)
    o = pl.pallas_call(
        functools.partial(_moba_kernel, L=L),
        out_shape=jax.ShapeDtypeStruct((bsz, L, D), F32),
        grid=(H, bsz, nb),
        in_specs=[pl.BlockSpec((None, L, dh), lambda h, b, n: (b, 0, h)),
                  pl.BlockSpec((None, L, dh), lambda h, b, n: (b, 0, H + h)),
                  pl.BlockSpec((None, L, dh), lambda h, b, n: (b, 0, 2 * H + h)),
                  pl.BlockSpec((None, 1, rev_len), lambda h, b, n: (h, 0, 0))],
        out_specs=pl.BlockSpec((None, BS, dh), lambda h, b, n: (b, n, h)),
        scratch_shapes=[pltpu.VMEM((LANES, dh), F32),
                        pltpu.VMEM((L, LANES), BF16),
                        pltpu.VMEM((nb, BS, BS), F32),
                        pltpu.VMEM((nb, BS, BS), F32),
                        pltpu.VMEM((BS, BS // 2), F32),
                        pltpu.VMEM((BS, BS // 2), F32),
                        pltpu.VMEM((BS, dh), F32)],
        compiler_params=_cparams(("parallel", "arbitrary", "arbitrary")),
        name="moba",
    )(proj, proj, proj, bias_rev)
    return _matmul_residual(o.reshape(T, D), w_out, x)


def kernel(x, norm_g, final_norm_g, ffn_w_gate, ffn_w_up, ffn_w_down, s5_lam_re, s5_lam_im, s5_log_dt, s5_b_re, s5_b_im, s5_c_re, s5_c_im, s5_d, s5_w_glu, hg_w_in, hg_lower_bounds, hg_norm_g, hg_w_out, gdn_w_in, gdn_conv_w, gdn_a_log, gdn_dt_bias, gdn_norm_g, gdn_w_out, moba_w_in, moba_w_out, rel_bias_table):
    bsz, L, D = x.shape
    depth = norm_g.shape[0]
    h = x.reshape(bsz * L, D)
    wg, wu, wd = ffn_w_gate.astype(BF16), ffn_w_up.astype(BF16), ffn_w_down.astype(BF16)
    for i in range(depth):
        m, j = i % 4, i // 4
        h = _ffn(h, norm_g[i, 0], wg, wu, wd, i, 0)
        if m == 0:
            h = _s5_mixer(h, norm_g[i, 1], s5_lam_re[j], s5_lam_im[j], s5_log_dt[j], s5_b_re[j],
                          s5_b_im[j], s5_c_re[j], s5_c_im[j], s5_d[j], s5_w_glu[j], bsz)
        elif m == 1:
            h = _hgrn_mixer(h, norm_g[i, 1], hg_w_in[j], hg_lower_bounds, hg_norm_g[j], hg_w_out[j],
                            bsz, i)
        elif m == 2:
            h = _gdn_mixer(h, norm_g[i, 1], gdn_w_in[j], gdn_conv_w[j], gdn_a_log[j], gdn_dt_bias[j],
                           gdn_norm_g[j], gdn_w_out[j], bsz)
        else:
            h = _moba_mixer(h, norm_g[i, 1], moba_w_in[j], rel_bias_table, moba_w_out[j], bsz)
        h = _ffn(h, norm_g[i, 2], wg, wu, wd, i, 1, final_norm_g if i == depth - 1 else None)
    return h.reshape(bsz, L, D)
```

```python
import functools
import math

import jax
import jax.numpy as jnp
import numpy as np
from jax import lax
from jax.experimental import pallas as pl
from jax.experimental.pallas import tpu as pltpu

F32 = jnp.float32
BF16 = jnp.bfloat16
LANES = 128
RMS_EPS = 1e-6

HG_HEADS = 8
HG_CHUNK = 32
GDN_QK_HEADS = 8
GDN_V_HEADS = 16
GDN_CHUNK = 64
GDN_CONV = 4
MOBA_HEADS = 8
MOBA_BLOCK = 256
MOBA_TOPK = 3
REL_BUCKETS = 32
REL_MAX_DIST = 2048
S5_GROUP = 16
S5_STATE = 64
S5_TC = 16
S5_GP = LANES // S5_GROUP

VMEM_LIMIT = 56 * 1024 * 1024


def _cparams(sem):
    return pltpu.CompilerParams(dimension_semantics=sem, vmem_limit_bytes=VMEM_LIMIT)


def _mm(a, b):
    return jnp.dot(a.astype(BF16), b.astype(BF16), preferred_element_type=F32)


def _mm_nt(a, b):
    return lax.dot_general(a.astype(BF16), b.astype(BF16), (((1,), (1,)), ((), ())),
                           preferred_element_type=F32)


def _mm_tn(a, b):
    return lax.dot_general(a.astype(BF16), b.astype(BF16), (((0,), (0,)), ((), ())),
                           preferred_element_type=F32)


def _mm_hi(a, b):
    return jnp.dot(a, b, precision=lax.Precision.HIGHEST, preferred_element_type=F32)


def _mm_nt_hi(a, b):
    return lax.dot_general(a, b, (((1,), (1,)), ((), ())), precision=lax.Precision.HIGHEST,
                           preferred_element_type=F32)


def _sigmoid(x):
    return 1.0 / (1.0 + jnp.exp(-x))


def _silu(x):
    return x * _sigmoid(x)


def _rms_rows(x, g):
    return x * lax.rsqrt(jnp.mean(x * x, axis=-1, keepdims=True) + RMS_EPS) * g


FFN_TM = 512
FFN_TF = 256


def _ffn_kernel(x_ref, g_ref, wg_ref, wu_ref, wd_ref, *rest, final_norm):
    o_ref = rest[-1]
    x = x_ref[...]
    h = _rms_rows(x, g_ref[...]).astype(BF16)
    acc = None
    for c in range(wg_ref.shape[1] // FFN_TF):
        cols = slice(c * FFN_TF, (c + 1) * FFN_TF)
        gate = jnp.dot(h, wg_ref[:, cols], preferred_element_type=F32)
        up = jnp.dot(h, wu_ref[:, cols], preferred_element_type=F32)
        a = (_silu(gate) * up).astype(BF16)
        part = jnp.dot(a, wd_ref[cols, :], preferred_element_type=F32)
        acc = part if acc is None else acc + part
    y = x + 0.5 * acc
    o_ref[...] = _rms_rows(y, rest[0][...]) if final_norm else y


def _ffn(x, g, wg, wu, wd, layer, which, final_g=None):
    T, D = x.shape
    F = wg.shape[-1]
    tm = min(FFN_TM, T)
    row = pl.BlockSpec((1, D), lambda i: (0, 0))
    resident = pl.Buffered(1)
    in_specs = [pl.BlockSpec((tm, D), lambda i: (i, 0)), row,
                pl.BlockSpec((None, None, D, F), lambda i: (layer, which, 0, 0), pipeline_mode=resident),
                pl.BlockSpec((None, None, D, F), lambda i: (layer, which, 0, 0), pipeline_mode=resident),
                pl.BlockSpec((None, None, F, D), lambda i: (layer, which, 0, 0), pipeline_mode=resident)]
    args = [x, g.reshape(1, D), wg, wu, wd]
    if final_g is not None:
        in_specs.append(row)
        args.append(final_g.reshape(1, D))
    return pl.pallas_call(
        functools.partial(_ffn_kernel, final_norm=final_g is not None),
        out_shape=jax.ShapeDtypeStruct((T, D), F32),
        grid=(T // tm,),
        in_specs=in_specs,
        out_specs=pl.BlockSpec((tm, D), lambda i: (i, 0)),
        compiler_params=_cparams(("parallel",)),
        name="ffn",
    )(*args)


PROJ_TM = 512


def _rmsmm_kernel(x_ref, g_ref, w_ref, o_ref, *, tn):
    h = _rms_rows(x_ref[...], g_ref[...]).astype(BF16)
    for c in range(w_ref.shape[1] // tn):
        cols = slice(c * tn, (c + 1) * tn)
        o_ref[:, cols] = jnp.dot(h, w_ref[:, cols], preferred_element_type=F32)


def _rms_matmul(x, g, w, tn):
    T, D = x.shape
    N = w.shape[1]
    tm = min(PROJ_TM, T)
    return pl.pallas_call(
        functools.partial(_rmsmm_kernel, tn=tn),
        out_shape=jax.ShapeDtypeStruct((T, N), F32),
        grid=(T // tm,),
        in_specs=[pl.BlockSpec((tm, D), lambda i: (i, 0)),
                  pl.BlockSpec((1, D), lambda i: (0, 0)),
                  pl.BlockSpec((D, N), lambda i: (0, 0), pipeline_mode=pl.Buffered(1))],
        out_specs=pl.BlockSpec((tm, N), lambda i: (i, 0)),
        compiler_params=_cparams(("parallel",)),
        name="rms_matmul",
    )(x, g.reshape(1, D), w.astype(BF16))


def _mmres_kernel(a_ref, w_ref, x_ref, o_ref):
    o_ref[...] = x_ref[...] + jnp.dot(a_ref[...].astype(BF16), w_ref[...],
                                      preferred_element_type=F32)


def _matmul_residual(a, w, x):
    T, K = a.shape
    N = w.shape[1]
    tm = min(PROJ_TM, T)
    return pl.pallas_call(
        _mmres_kernel,
        out_shape=jax.ShapeDtypeStruct((T, N), F32),
        grid=(T // tm,),
        in_specs=[pl.BlockSpec((tm, K), lambda i: (i, 0)),
                  pl.BlockSpec((K, N), lambda i: (0, 0), pipeline_mode=pl.Buffered(1)),
                  pl.BlockSpec((tm, N), lambda i: (i, 0))],
        out_specs=pl.BlockSpec((tm, N), lambda i: (i, 0)),
        compiler_params=_cparams(("parallel",)),
        name="matmul_residual",
    )(a, w.astype(BF16), x)


def _cmul(ar, ai, br, bi):
    return ar * br - ai * bi, ar * bi + ai * br


def _s5_powers(lr, li, ldt, n):
    lr = jnp.minimum(lr, -1e-4)
    dt = jnp.exp(ldt)
    mag = jnp.exp(lr * dt)
    br = mag * jnp.cos(li * dt)
    bi = mag * jnp.sin(li * dt)
    den = lr * lr + li * li
    fr = ((br - 1.0) * lr + bi * li) / den
    fi = (bi * lr - (br - 1.0) * li) / den
    pw = [(jnp.ones_like(br), jnp.zeros_like(bi))]
    for _ in range(n):
        pw.append(_cmul(pw[-1][0], pw[-1][1], br, bi))
    return pw, (fr, fi)


def _s5_prep_kernel(lr_row, li_row, ldt_row, lr_col, li_col, ldt_col, btr_ref, bti_ref, ctr_ref,
                    cti_ref, bd_ref, bexp_ref, cexp_ref, a_ref):
    tc = S5_TC
    ns = lr_row.shape[-1]
    pw_row, (fr, fi) = _s5_powers(lr_row[...], li_row[...], ldt_row[...], tc)
    pw_col, _ = _s5_powers(lr_col[...], li_col[...], ldt_col[...], tc)
    btr, bti = btr_ref[...], bti_ref[...]
    ctr, cti = ctr_ref[...], cti_ref[...]
    for k in range(tc):
        fer, fei = _cmul(fr, fi, pw_row[k][0], pw_row[k][1])
        xr = btr * fer - bti * fei
        xi = btr * fei + bti * fer
        bd_ref[k] = (_mm_hi(xr, ctr) - _mm_hi(xi, cti)).astype(BF16)
        s = tc - 1 - k
        bexp_ref[s * LANES:(s + 1) * LANES, 0:ns] = xr.astype(BF16)
        bexp_ref[s * LANES:(s + 1) * LANES, ns:2 * ns] = xi.astype(BF16)
        er, ei = pw_col[k + 1]
        cexp_ref[0:ns, k * LANES:(k + 1) * LANES] = (ctr * er - cti * ei).astype(BF16)
        cexp_ref[ns:2 * ns, k * LANES:(k + 1) * LANES] = (-(ctr * ei + cti * er)).astype(BF16)
    a_ref[:, 0:ns] = pw_row[tc][0]
    a_ref[:, ns:2 * ns] = pw_row[tc][1]


def _s5_main_kernel(u_ref, bd_ref, bexp_ref, cexp_ref, a_ref, y_ref, m_ref, hl_ref, hp_ref):
    tc = S5_TC
    ns = a_ref.shape[-1] // 2
    rows = u_ref.shape[0]

    @pl.when(pl.program_id(1) == 0)
    def _():
        for s in range(tc):
            for t in range(tc):
                blk = bd_ref[t - s] if t >= s else jnp.zeros((LANES, LANES), BF16)
                m_ref[s * LANES:(s + 1) * LANES, t * LANES:(t + 1) * LANES] = blk

    u = u_ref[...].astype(BF16)
    hl_ref[...] = jnp.dot(u, bexp_ref[...], preferred_element_type=F32)
    are, aim = a_ref[:, 0:ns], a_ref[:, ns:2 * ns]

    def step(c, carry):
        hre, him = carry
        hp_ref[pl.ds(c, 1), 0:ns] = hre
        hp_ref[pl.ds(c, 1), ns:2 * ns] = him
        xre = hl_ref[pl.ds(c, 1), 0:ns]
        xim = hl_ref[pl.ds(c, 1), ns:2 * ns]
        return are * hre - aim * him + xre, are * him + aim * hre + xim

    z = jnp.zeros((1, ns), F32)
    lax.fori_loop(0, rows, step, (z, z))
    hp = hp_ref[...].astype(BF16)
    wide = 2 * LANES
    for t2 in range(tc * LANES // wide):
        cols = slice(t2 * wide, (t2 + 1) * wide)
        kk = (t2 + 1) * wide
        y_ref[:, cols] = (jnp.dot(u[:, 0:kk], m_ref[0:kk, cols], preferred_element_type=F32)
                          + jnp.dot(hp, cexp_ref[:, cols], preferred_element_type=F32))


def _s5_pre_kernel(x_ref, g_ref, ug_ref, u_ref):
    tc = S5_TC
    ng, r, _ = ug_ref.shape
    u = _rms_rows(x_ref[...], g_ref[...])
    for n in range(ng):
        u_ref[n] = u[:, n * LANES:(n + 1) * LANES]
    for n in range(ng):
        for t in range(tc):
            ug_ref[n, :, t * LANES:(t + 1) * LANES] = (
                u_ref[n, pl.ds(t, r, stride=tc), :].astype(BF16))


S5_POST_TM = 512
S5_POST_SUB = 256
S5_POST_TN = 512


def _s5_post_kernel(yg_ref, x_ref, g_ref, d_ref, w_ref, o_ref, y_ref):
    tc = S5_TC
    ng = yg_ref.shape[0]
    tm, D = x_ref.shape
    sub = min(S5_POST_SUB, tm)
    rs = sub // tc
    c = math.sqrt(2.0 / math.pi)
    for s in range(tm // sub):
        rows = slice(s * sub, (s + 1) * sub)
        for n in range(ng):
            for t in range(tc):
                y_ref[s * ng + n, pl.ds(t, rs, stride=tc), :] = (
                    yg_ref[n, s * rs:(s + 1) * rs, t * LANES:(t + 1) * LANES])
        x = x_ref[rows, :]
        inv_rms = lax.rsqrt(jnp.mean(x * x, axis=-1, keepdims=True) + RMS_EPS)
        strips = []
        for n in range(ng):
            cols = slice(n * LANES, (n + 1) * LANES)
            y = y_ref[s * ng + n] + d_ref[:, cols] * (x[:, cols] * inv_rms * g_ref[:, cols])
            strips.append((0.5 * y * (1.0 + jnp.tanh(c * (y + 0.044715 * (y * y * y))))).astype(BF16))
        a = jnp.concatenate(strips, axis=1)
        for j in range(D // S5_POST_TN):
            cols = slice(j * S5_POST_TN, (j + 1) * S5_POST_TN)
            gcols = slice(D + j * S5_POST_TN, D + (j + 1) * S5_POST_TN)
            val = jnp.dot(a, w_ref[:, cols], preferred_element_type=F32)
            gate = jnp.dot(a, w_ref[:, gcols], preferred_element_type=F32)
            o_ref[rows, cols] = x[:, cols] + val * _sigmoid(gate)


def _s5_mixer(x, g, lam_re, lam_im, log_dt, b_re, b_im, c_re, c_im, d_skip, w_glu, bsz):
    T, D = x.shape
    L = T // bsz
    G, P = lam_re.shape
    tc, gp = S5_TC, S5_GP
    ng = G // gp
    ns = gp * P
    rows = L // tc

    eye = jnp.eye(gp, dtype=F32)

    def bt(b):
        return jnp.einsum('ngph,gk->nghkp', b.reshape(ng, gp, P, S5_GROUP), eye).reshape(ng, LANES, ns)

    def ct(c):
        return jnp.einsum('nghp,gk->ngpkh', c.reshape(ng, gp, S5_GROUP, P), eye).reshape(ng, ns, LANES)

    lr, li = lam_re.reshape(ng, 1, ns), lam_im.reshape(ng, 1, ns)
    ldt = jnp.broadcast_to(log_dt[:, None], (G, P)).reshape(ng, 1, ns)
    row_spec = pl.BlockSpec((None, 1, ns), lambda n: (n, 0, 0))
    col_spec = pl.BlockSpec((None, ns, 1), lambda n: (n, 0, 0))
    bd, bexp, cexp, a16 = pl.pallas_call(
        _s5_prep_kernel,
        out_shape=(jax.ShapeDtypeStruct((ng, tc, LANES, LANES), BF16),
                   jax.ShapeDtypeStruct((ng, tc * LANES, 2 * ns), BF16),
                   jax.ShapeDtypeStruct((ng, 2 * ns, tc * LANES), BF16),
                   jax.ShapeDtypeStruct((ng, 1, 2 * ns), F32)),
        grid=(ng,),
        in_specs=[row_spec, row_spec, row_spec, col_spec, col_spec, col_spec,
                  pl.BlockSpec((None, LANES, ns), lambda n: (n, 0, 0)),
                  pl.BlockSpec((None, LANES, ns), lambda n: (n, 0, 0)),
                  pl.BlockSpec((None, ns, LANES), lambda n: (n, 0, 0)),
                  pl.BlockSpec((None, ns, LANES), lambda n: (n, 0, 0))],
        out_specs=(pl.BlockSpec((None, tc, LANES, LANES), lambda n: (n, 0, 0, 0)),
                   pl.BlockSpec((None, tc * LANES, 2 * ns), lambda n: (n, 0, 0)),
                   pl.BlockSpec((None, 2 * ns, tc * LANES), lambda n: (n, 0, 0)),
                   pl.BlockSpec((None, 1, 2 * ns), lambda n: (n, 0, 0))),
        compiler_params=_cparams(("parallel",)),
        name="s5_prep",
    )(lr, li, ldt, lr.reshape(ng, ns, 1), li.reshape(ng, ns, 1), ldt.reshape(ng, ns, 1),
      bt(b_re), bt(b_im), ct(c_re), ct(c_im))

    tm = min(1024, T)
    ug = pl.pallas_call(
        _s5_pre_kernel,
        out_shape=jax.ShapeDtypeStruct((ng, T // tc, tc * LANES), BF16),
        grid=(T // tm,),
        in_specs=[pl.BlockSpec((tm, D), lambda i: (i, 0)), pl.BlockSpec((1, D), lambda i: (0, 0))],
        out_specs=pl.BlockSpec((ng, tm // tc, tc * LANES), lambda i: (0, i, 0)),
        scratch_shapes=[pltpu.VMEM((ng, tm, LANES), F32)],
        compiler_params=_cparams(("parallel",)),
        name="s5_pre",
    )(x, g.reshape(1, D))
    yg = pl.pallas_call(
        _s5_main_kernel,
        out_shape=jax.ShapeDtypeStruct((ng, T // tc, tc * LANES), F32),
        grid=(ng, bsz),
        in_specs=[pl.BlockSpec((None, rows, tc * LANES), lambda n, b: (n, b, 0)),
                  pl.BlockSpec((None, tc, LANES, LANES), lambda n, b: (n, 0, 0, 0)),
                  pl.BlockSpec((None, tc * LANES, 2 * ns), lambda n, b: (n, 0, 0)),
                  pl.BlockSpec((None, 2 * ns, tc * LANES), lambda n, b: (n, 0, 0)),
                  pl.BlockSpec((None, 1, 2 * ns), lambda n, b: (n, 0, 0))],
        out_specs=pl.BlockSpec((None, rows, tc * LANES), lambda n, b: (n, b, 0)),
        scratch_shapes=[pltpu.VMEM((tc * LANES, tc * LANES), BF16),
                        pltpu.VMEM((rows, 2 * ns), F32),
                        pltpu.VMEM((rows, 2 * ns), F32)],
        compiler_params=_cparams(("parallel", "arbitrary")),
        name="s5_main",
    )(ug, bd, bexp, cexp, a16)

    tp = min(S5_POST_TM, T)
    sub = min(S5_POST_SUB, tp)
    return pl.pallas_call(
        _s5_post_kernel,
        out_shape=jax.ShapeDtypeStruct((T, D), F32),
        grid=(T // tp,),
        in_specs=[pl.BlockSpec((ng, tp // tc, tc * LANES), lambda i: (0, i, 0)),
                  pl.BlockSpec((tp, D), lambda i: (i, 0)),
                  pl.BlockSpec((1, D), lambda i: (0, 0)),
                  pl.BlockSpec((1, D), lambda i: (0, 0)),
                  pl.BlockSpec((D, 2 * D), lambda i: (0, 0), pipeline_mode=pl.Buffered(1))],
        out_specs=pl.BlockSpec((tp, D), lambda i: (i, 0)),
        scratch_shapes=[pltpu.VMEM((tp // sub * ng, sub, LANES), F32)],
        compiler_params=_cparams(("parallel",)),
        name="s5_post",
    )(yg, x, g.reshape(1, D), d_skip.reshape(1, D), w_glu.astype(BF16))


HG_SUPER = 256


HG_HEADS_PER_STEP = 4
HG_LTILE = 1024


def _hgrn_kernel(q_ref, f_ref, i_ref, g_ref, lb_ref, ng_ref, o_ref, st_ref, *, layer_idx):
    Lt = q_ref.shape[0]
    hp = HG_HEADS_PER_STEP
    dk = q_ref.shape[1] // hp
    C, SC = HG_CHUNK, HG_SUPER
    lbraw = lb_ref[...]
    e = jnp.exp(lbraw - jnp.max(lbraw, axis=0, keepdims=True))
    sm = e / jnp.sum(e, axis=0, keepdims=True)
    layer = lax.broadcasted_iota(jnp.int32, sm.shape, 0)
    lb = jnp.sum(jnp.where((layer >= 1) & (layer <= layer_idx), sm, 0.0), axis=0, keepdims=True)

    ti = lax.broadcasted_iota(jnp.int32, (SC, SC), 0)
    si = lax.broadcasted_iota(jnp.int32, (SC, SC), 1)
    causal = ((ti // C) == (si // C)) & (si <= ti)
    pos = lax.broadcasted_iota(jnp.int32, (SC, hp * dk), 0) & (C - 1)
    ng = ng_ref[...]

    @pl.when(pl.program_id(2) == 0)
    def _():
        st_ref[...] = jnp.zeros_like(st_ref)

    def body(sc, _):
        r0 = pl.multiple_of(sc * SC, SC)
        f = f_ref[pl.ds(r0, SC), :]
        fgate = lb + (1.0 - lb) * _sigmoid(f)
        logf = jnp.log(fgate)
        k = (1.0 - lb) * _sigmoid(-f)
        q = _silu(q_ref[pl.ds(r0, SC), :]) * dk ** -0.5
        v = i_ref[pl.ds(r0, SC), :]
        b, rest = logf, logf
        shift = 1
        while shift < C:
            b = b + jnp.where(pos >= shift, pltpu.roll(b, shift, 0), 0.0)
            rest = rest + jnp.where(pos + shift < C, pltpu.roll(rest, SC - shift, 0), 0.0)
            shift *= 2
        q_t = (q * jnp.exp(b)).astype(BF16)
        k_t = (k * jnp.exp(-b)).astype(BF16)
        k_state = (k * jnp.exp(rest - logf)).astype(BF16)
        vb = v.astype(BF16)
        nc = SC // C
        heads = range(hp)

        def hs(t, h, c=None):
            t = t[:, h * dk:(h + 1) * dk]
            return t if c is None else t[c * C:(c + 1) * C]

        attn = [jnp.where(causal, _mm_nt(hs(q_t, h), hs(k_t, h)), 0.0) for h in heads]
        outer = [[_mm_tn(hs(vb, h, c), hs(k_state, h, c)) for c in range(nc)] for h in heads]
        intra = [_mm(attn[h], hs(vb, h)) for h in heads]
        states = [[st_ref[h]] for h in heads]
        for c in range(nc):
            dc = jnp.exp(b[(c + 1) * C - 1:(c + 1) * C, :])
            for h in heads:
                states[h].append(states[h][c] * hs(dc, h) + outer[h][c])
        for h in heads:
            st_ref[h] = states[h][nc]
        gs = _silu(g_ref[pl.ds(r0, SC), :])
        for h in heads:
            inter = [_mm_nt(hs(q_t, h, c), states[h][c]) for c in range(nc)]
            o = intra[h] + jnp.concatenate(inter, axis=0)
            o_ref[pl.ds(r0, SC), h * dk:(h + 1) * dk] = _rms_rows(o, ng) * hs(gs, h)
        return 0

    lax.fori_loop(0, Lt // SC, body, 0)


def _hgrn_mixer(x, g, w_in, lower_bounds, norm_g, w_out, bsz, layer_idx):
    T, D = x.shape
    L = T // bsz
    H = HG_HEADS
    dk = D // H
    proj = _rms_matmul(x, g, w_in, 512).reshape(bsz, L, 4 * D)
    nl = lower_bounds.shape[0]

    hp = HG_HEADS_PER_STEP
    ngrp = H // hp
    lt = min(HG_LTILE, L)

    def col(off):
        return pl.BlockSpec((None, lt, hp * dk), lambda b, h, t: (b, t, off * ngrp + h))

    o = pl.pallas_call(
        functools.partial(_hgrn_kernel, layer_idx=layer_idx),
        out_shape=jax.ShapeDtypeStruct((bsz, L, D), F32),
        grid=(bsz, ngrp, L // lt),
        in_specs=[col(0), col(1), col(2), col(3),
                  pl.BlockSpec((nl, hp * dk), lambda b, h, t: (0, h)),
                  pl.BlockSpec((1, dk), lambda b, h, t: (0, 0))],
        out_specs=pl.BlockSpec((None, lt, hp * dk), lambda b, h, t: (b, t, h)),
        scratch_shapes=[pltpu.VMEM((hp, dk, dk), F32)],
        compiler_params=_cparams(("parallel", "parallel", "arbitrary")),
        name="hgrn",
    )(proj, proj, proj, proj, lower_bounds, norm_g.reshape(1, dk))
    return _matmul_residual(o.reshape(T, D), w_out, x)


GDN_LTILE = 512
GDN_QK_PER_STEP = 4
GDN_SOLVE_CHUNKS = 2
GDN_SOLVE_WIDE_STEPS = 1


def _split3(x):
    x1 = x.astype(BF16)
    r = x - x1.astype(F32)
    x2 = r.astype(BF16)
    return x1, x2, (r - x2.astype(F32)).astype(BF16)


def _mm_sel(x, sel):
    s = sel.astype(BF16)
    p1, p2, p3 = (jnp.dot(p, s, preferred_element_type=F32) for p in _split3(x))
    return p1 + (p2 + p3)


def _mm_sel_nt(sel, x):
    s = sel.astype(BF16)
    p1, p2, p3 = (lax.dot_general(s, p, (((1,), (1,)), ((), ())), preferred_element_type=F32)
                  for p in _split3(x))
    return p1 + (p2 + p3)


def _mm3(a, b):
    a1 = a.astype(BF16)
    a2 = (a - a1.astype(F32)).astype(BF16)
    b1 = b.astype(BF16)
    b2 = (b - b1.astype(F32)).astype(BF16)
    return (jnp.dot(a1, b1, preferred_element_type=F32)
            + (jnp.dot(a1, b2, preferred_element_type=F32) + jnp.dot(a2, b1, preferred_element_type=F32)))


def _gdn_kernel(q_ref, k_ref, v_ref, z_ref, gc_ref, cwq_ref, cwk_ref, cwv_ref, alog_ref, dtb_ref,
                ng_ref, o_ref, xq_ref, xk_ref, xv_ref, qn_ref, kn_ref, vc_ref, bcol_ref, gcol_ref,
                beta_ref, bcum_ref, u_ref, w_ref, qk_ref, s_ref):
    Lt = q_ref.shape[0]
    dk = LANES
    C = GDN_CHUNK
    nqk = GDN_QK_PER_STEP
    nh = nqk * (GDN_V_HEADS // GDN_QK_HEADS)
    hg = pl.program_id(1)
    first = pl.program_id(2) == 0
    pad = 8

    @pl.when(first)
    def _():
        s_ref[...] = jnp.zeros_like(s_ref)
        xq_ref[0:pad, :] = jnp.zeros((pad, xq_ref.shape[1]), F32)
        xk_ref[0:pad, :] = jnp.zeros((pad, xk_ref.shape[1]), F32)
        xv_ref[0:pad, :] = jnp.zeros((pad, xv_ref.shape[1]), F32)

    def conv_silu(src_ref, w_ref, xp_ref):
        xp_ref[pad:pad + Lt, :] = src_ref[...]
        w = w_ref[...]
        acc = None
        for j in range(GDN_CONV):
            s = pad - (GDN_CONV - 1) + j
            term = w[j:j + 1, :] * xp_ref[s:s + Lt, :]
            acc = term if acc is None else acc + term
        xp_ref[0:pad, :] = xp_ref[Lt:Lt + pad, :]
        return _silu(acc)

    def l2n(t):
        return t * lax.rsqrt(jnp.sum(t * t, axis=-1, keepdims=True) + 1e-6)

    qa = conv_silu(q_ref, cwq_ref, xq_ref)
    ka = conv_silu(k_ref, cwk_ref, xk_ref)
    for i in range(nqk):
        qn_ref[:, i * dk:(i + 1) * dk] = l2n(qa[:, i * dk:(i + 1) * dk]) * dk ** -0.5
        kn_ref[:, i * dk:(i + 1) * dk] = l2n(ka[:, i * dk:(i + 1) * dk])
    vc_ref[...] = conv_silu(v_ref, cwv_ref, xv_ref)

    gcraw = gc_ref[...]
    bcol_ref[...] = _sigmoid(gcraw)
    xs = gcraw + dtb_ref[...]
    softplus = jnp.maximum(xs, 0.0) + jnp.log(1.0 + jnp.exp(-jnp.abs(xs)))
    gcol_ref[...] = -jnp.exp(alog_ref[...]) * softplus

    row = lax.broadcasted_iota(jnp.int32, (Lt, LANES), 0) & (C - 1)
    bc_all = gcol_ref[...]
    shift = 1
    while shift < C:
        bc_all = bc_all + jnp.where(row >= shift, pltpu.roll(bc_all, shift, 0), 0.0)
        shift *= 2
    sel_r = lax.broadcasted_iota(jnp.int32, (LANES, LANES), 0)
    braw = bcol_ref[...]
    for j in range(nh):
        hv = nh * hg + j
        beta_ref[j] = _mm_sel(braw, jnp.where(sel_r == hv, 1.0, 0.0))
        bcum_ref[j] = _mm_sel(bc_all, jnp.where(sel_r == hv + GDN_V_HEADS, 1.0, 0.0))

    ti = lax.broadcasted_iota(jnp.int32, (C, C), 0)
    si = lax.broadcasted_iota(jnp.int32, (C, C), 1)
    incl = si <= ti
    strict = si < ti
    eye_c = jnp.where(si == ti, 1.0, 0.0)
    ng = ng_ref[...]
    rep = nh // nqk

    def solve_body(cc, _):
        rows = [pl.ds(pl.multiple_of((cc * GDN_SOLVE_CHUNKS + a) * C, C), C)
                for a in range(GDN_SOLVE_CHUNKS)]
        kk, qk0 = {}, {}
        for a, rw in enumerate(rows):
            for i in range(nqk):
                kc = kn_ref[rw, i * dk:(i + 1) * dk]
                kk[a, i] = _mm_nt(kc, kc)
                qk0[a, i] = _mm_nt(qn_ref[rw, i * dk:(i + 1) * dk], kc)
        probs = [(a, j) for a in range(GDN_SOLVE_CHUNKS) for j in range(nh)]
        pmat, sol, decay = {}, {}, {}
        for a, j in probs:
            i = j // rep
            bcum = bcum_ref[j, rows[a], :]
            beta = beta_ref[j, rows[a], :]
            dmat = bcum[:, 0:C] - jnp.transpose(bcum)[0:1, :]
            decay[a, j] = jnp.where(incl, jnp.exp(jnp.where(incl, dmat, 0.0)), 0.0)
            pmat[a, j] = jnp.where(strict, -(kk[a, i] * beta[:, 0:C]) * decay[a, j], 0.0)
            sol[a, j] = jnp.concatenate(
                [vc_ref[rows[a], j * dk:(j + 1) * dk] * beta,
                 kn_ref[rows[a], i * dk:(i + 1) * dk] * beta * jnp.exp(bcum)], axis=1)
        inv = {p: eye_c + pmat[p] for p in probs}
        pmat = {p: _mm3(pmat[p], pmat[p]) for p in probs}
        for it in range(1, 6):
            mm = _mm3 if it < GDN_SOLVE_WIDE_STEPS else _mm
            if it < 5:
                prod = {p: mm(pmat[p], jnp.concatenate([inv[p], pmat[p]], axis=1)) for p in probs}
                inv = {p: inv[p] + prod[p][:, 0:C] for p in probs}
                pmat = {p: prod[p][:, C:2 * C] for p in probs}
            else:
                inv = {p: inv[p] + mm(pmat[p], inv[p]) for p in probs}
        sol = {p: _mm3(inv[p], sol[p]) for p in probs}
        for a, j in probs:
            u_ref[j, rows[a], :] = sol[a, j][:, 0:dk]
            w_ref[j, rows[a], :] = sol[a, j][:, dk:2 * dk]
            qk_ref[j, rows[a], :] = qk0[a, j // rep] * decay[a, j]
        return 0

    lax.fori_loop(0, Lt // (C * GDN_SOLVE_CHUNKS), solve_body, 0)

    def state_body(c, _):
        r0 = pl.multiple_of(c * C, C)
        rows = pl.ds(r0, C)
        heads = range(nh)
        S, blast, k_state, ws = [], [], [], []
        for j in heads:
            i = j // rep
            bcum = bcum_ref[j, rows, :]
            blast.append(bcum_ref[j, pl.ds(r0 + C - 1, 1), :])
            q_dec = qn_ref[rows, i * dk:(i + 1) * dk] * jnp.exp(bcum)
            k_state.append(kn_ref[rows, i * dk:(i + 1) * dk] * jnp.exp(blast[j] - bcum))
            S.append(s_ref[j])
            ws.append(_mm(jnp.concatenate([w_ref[j, rows, :], q_dec], axis=0), S[j]))
        v_new = [u_ref[j, rows, :] - ws[j][0:C] for j in heads]
        for j in heads:
            s_ref[j] = S[j] * jnp.exp(blast[j]) + _mm_tn(k_state[j], v_new[j])
        for j in heads:
            o = ws[j][C:2 * C] + _mm(qk_ref[j, rows, :], v_new[j])
            z = z_ref[rows, j * dk:(j + 1) * dk]
            o_ref[rows, j * dk:(j + 1) * dk] = _rms_rows(o, ng) * _silu(z)
        return 0

    lax.fori_loop(0, Lt // C, state_body, 0)


def _gdn_mixer(x, g, w_in, conv_w, a_log, dt_bias, norm_g, w_out, bsz):
    T, D = x.shape
    L = T // bsz
    Hk, Hv = GDN_QK_HEADS, GDN_V_HEADS
    dk = LANES
    nqk = GDN_QK_PER_STEP
    nh = nqk * (Hv // Hk)
    ngrp = Hk // nqk
    qw, vw = nqk * dk, nh * dk
    lt = min(GDN_LTILE, L)
    n_in = w_in.shape[1]
    n_pad = -(-n_in // (7 * LANES)) * (7 * LANES)
    w_pad = jnp.pad(w_in, ((0, 0), (0, n_pad - n_in)))
    proj = _rms_matmul(x, g, w_pad, 7 * LANES).reshape(bsz, L, n_pad)
    qkv_w = 2 * Hk * dk + Hv * dk
    gate_blk = (qkv_w + Hv * dk) // LANES
    alog_row = jnp.zeros((1, LANES), F32).at[0, Hv:2 * Hv].set(a_log)
    dtb_row = jnp.zeros((1, LANES), F32).at[0, Hv:2 * Hv].set(dt_bias)
    o = pl.pallas_call(
        _gdn_kernel,
        out_shape=jax.ShapeDtypeStruct((bsz, L, Hv * dk), F32),
        grid=(bsz, ngrp, L // lt),
        in_specs=[pl.BlockSpec((None, lt, qw), lambda b, h, t: (b, t, h)),
                  pl.BlockSpec((None, lt, qw), lambda b, h, t: (b, t, ngrp + h)),
                  pl.BlockSpec((None, lt, vw), lambda b, h, t: (b, t, ngrp + h)),
                  pl.BlockSpec((None, lt, vw), lambda b, h, t: (b, t, 2 * ngrp + h)),
                  pl.BlockSpec((None, lt, LANES), lambda b, h, t: (b, t, gate_blk)),
                  pl.BlockSpec((GDN_CONV, qw), lambda b, h, t: (0, h)),
                  pl.BlockSpec((GDN_CONV, qw), lambda b, h, t: (0, ngrp + h)),
                  pl.BlockSpec((GDN_CONV, vw), lambda b, h, t: (0, ngrp + h)),
                  pl.BlockSpec((1, LANES), lambda b, h, t: (0, 0)),
                  pl.BlockSpec((1, LANES), lambda b, h, t: (0, 0)),
                  pl.BlockSpec((1, dk), lambda b, h, t: (0, 0))],
        out_specs=pl.BlockSpec((None, lt, vw), lambda b, h, t: (b, t, h)),
        scratch_shapes=[pltpu.VMEM((lt + 8, qw), F32),
                        pltpu.VMEM((lt + 8, qw), F32),
                        pltpu.VMEM((lt + 8, vw), F32),
                        pltpu.VMEM((lt, qw), F32),
                        pltpu.VMEM((lt, qw), F32),
                        pltpu.VMEM((lt, vw), F32),
                        pltpu.VMEM((lt, LANES), F32),
                        pltpu.VMEM((lt, LANES), F32),
                        pltpu.VMEM((nh, lt, LANES), F32),
                        pltpu.VMEM((nh, lt, LANES), F32),
                        pltpu.VMEM((nh, lt, dk), F32),
                        pltpu.VMEM((nh, lt, dk), F32),
                        pltpu.VMEM((nh, lt, GDN_CHUNK), F32),
                        pltpu.VMEM((nh, dk, dk), F32)],
        compiler_params=_cparams(("parallel", "parallel", "arbitrary")),
        name="gdn",
    )(proj, proj, proj, proj, proj, conv_w, conv_w, conv_w, alog_row, dtb_row, norm_g.reshape(1, dk))
    return _matmul_residual(o.reshape(T, Hv * dk), w_out, x)


def _t5_bucket_np(n):
    max_exact = REL_BUCKETS // 2
    nf = np.maximum(n, 1).astype(np.float32)
    large = max_exact + (np.log(nf / np.float32(max_exact)) / np.float32(math.log(REL_MAX_DIST / max_exact))
                         * np.float32(REL_BUCKETS - max_exact)).astype(np.int32)
    large = np.minimum(large, REL_BUCKETS - 1)
    return np.where(n < max_exact, n, large).astype(np.int32)


def _bias_vec_kernel(bucket_ref, table_ref, o_ref):
    bucket = bucket_ref[...]
    table = table_ref[...]
    acc = jnp.zeros(o_ref.shape, F32)
    for k in range(REL_BUCKETS):
        acc = acc + jnp.where(bucket == k, table[:, k:k + 1], 0.0)
    o_ref[...] = acc


MOBA_MASK = 1e30
MOBA_UNROLL = 4


def _moba_kernel(q_ref, k_ref, v_ref, bias_ref, o_ref, kmean_ref, mask_ref, tiles_ref, s_ref, mx_ref,
                 ls_ref, acc_ref, *, L):
    BS = MOBA_BLOCK
    dh = q_ref.shape[-1]
    nb = L // BS
    qb = pl.program_id(2)
    rev_len = bias_ref.shape[-1]

    @pl.when((pl.program_id(1) == 0) & (qb == 0))
    def _():
        for d in range(nb):
            start = rev_len - 2 * BS - d * BS
            wb = jnp.broadcast_to(bias_ref[:, start:start + 2 * BS], (BS, 2 * BS))
            tiles_ref[d] = pltpu.roll(wb, BS + 1, 1, stride=1, stride_axis=0)[:, 0:BS]

    @pl.when(qb == 0)
    def _():
        kmean_ref[...] = jnp.zeros_like(kmean_ref)
        for n in range(nb):
            kmean_ref[n:n + 1, :] = jnp.sum(k_ref[n * BS:(n + 1) * BS, :], axis=0, keepdims=True) / BS
        blk_row = lax.broadcasted_iota(jnp.int32, (nb, L), 0)
        q_blk = lax.broadcasted_iota(jnp.int32, (nb, L), 1) // BS
        past = blk_row < q_blk
        gate = jnp.where(past, _mm_nt_hi(kmean_ref[0:nb, :], q_ref[...]), -1e30)
        sel_t = jnp.zeros((nb, L), F32)
        for _ in range(MOBA_TOPK):
            mx = jnp.max(gate, axis=0, keepdims=True)
            first = jnp.min(jnp.where(gate == mx, blk_row, nb), axis=0, keepdims=True)
            pick = blk_row == first
            sel_t = jnp.where(pick & past, 1.0, sel_t)
            gate = jnp.where(pick, -jnp.inf, gate)
        eye = jnp.where(lax.broadcasted_iota(jnp.int32, (nb, LANES), 0)
                        == lax.broadcasted_iota(jnp.int32, (nb, LANES), 1), 1.0, 0.0)
        mask_ref[...] = _mm_tn((sel_t - 1.0) * MOBA_MASK, eye).astype(BF16)

    scale = dh ** -0.5
    own0 = pl.multiple_of(qb * BS, BS)
    qbf = q_ref[pl.ds(own0, BS), :].astype(BF16)
    qm = jnp.concatenate([qbf, mask_ref[pl.ds(own0, BS), :]], axis=1)
    half = BS // 2

    def fold(t):
        return jnp.maximum(t[:, 0:half], t[:, half:BS])

    qi = lax.broadcasted_iota(jnp.int32, (BS, BS), 0)
    ki = lax.broadcasted_iota(jnp.int32, (BS, BS), 1)
    s = _mm_nt(qbf, k_ref[pl.ds(own0, BS), :]) * scale + tiles_ref[0]
    s = jnp.where(ki <= qi, s, -jnp.inf)
    s_ref[qb] = s
    mx_ref[...] = fold(s)
    blk_lane = lax.broadcasted_iota(jnp.int32, (BS, LANES), 1)

    def logits(js):
        raw = []
        for j in js:
            kj = k_ref[pl.ds(pl.multiple_of(j * BS, BS), BS), :].astype(BF16)
            onehot = jnp.where(blk_lane == j, 1.0, 0.0).astype(BF16)
            raw.append(_mm_nt(qm, jnp.concatenate([kj, onehot], axis=1)))
        mx = mx_ref[...]
        for j, r in zip(js, raw):
            s = r * scale + tiles_ref[qb - j]
            s_ref[j] = s
            mx = jnp.maximum(mx, fold(s))
        mx_ref[...] = mx

    def unrolled(n, fn):
        def body(jj, _):
            fn([MOBA_UNROLL * jj + a for a in range(MOBA_UNROLL)])
            return 0

        lax.fori_loop(0, n // MOBA_UNROLL, body, 0)
        base = (n // MOBA_UNROLL) * MOBA_UNROLL
        step = MOBA_UNROLL // 2
        while step >= 1:
            @pl.when((n & step) != 0)
            def _(base=base, step=step):
                fn([base + a for a in range(step)])

            base = base + (n & step)
            step //= 2

    unrolled(qb, logits)
    m = jnp.max(mx_ref[...], axis=1, keepdims=True)

    ls_ref[...] = jnp.zeros_like(ls_ref)
    acc_ref[...] = jnp.zeros_like(acc_ref)

    def accumulate(js):
        ps = [jnp.exp(s_ref[j] - m) for j in js]
        pv = [_mm(p, v_ref[pl.ds(pl.multiple_of(j * BS, BS), BS), :]) for j, p in zip(js, ps)]
        ls, acc = ls_ref[...], acc_ref[...]
        for p, o in zip(ps, pv):
            ls = ls + (p[:, 0:half] + p[:, half:BS])
            acc = acc + o
        ls_ref[...] = ls
        acc_ref[...] = acc

    unrolled(qb + 1, accumulate)
    o_ref[...] = acc_ref[...] / jnp.sum(ls_ref[...], axis=1, keepdims=True)


def _moba_mixer(x, g, w_in, rel_table, w_out, bsz):
    T, D = x.shape
    L = T // bsz
    H, dh, BS = MOBA_HEADS, LANES, MOBA_BLOCK
    nb = L // BS
    proj = _rms_matmul(x, g, w_in, 512).reshape(bsz, L, 3 * D)
    rev_len = L + BS
    dist = np.maximum(L - 1 - np.arange(rev_len), 0)
    bucket_rev = jnp.asarray(_t5_bucket_np(dist)[None, :])
    bias_rev = pl.pallas_call(
        _bias_vec_kernel,
        out_shape=jax.ShapeDtypeStruct((H, rev_len), F32),
        name="moba_bias",
    )(bucket_rev, rel_table.T).reshape(H, 1, rev_len)
    o = pl.pallas_call(
        functools.partial(_moba_kernel, L=L),
        out_shape=jax.ShapeDtypeStruct((bsz, L, D), F32),
        grid=(H, bsz, nb),
        in_specs=[pl.BlockSpec((None, L, dh), lambda h, b, n: (b, 0, h)),
                  pl.BlockSpec((None, L, dh), lambda h, b, n: (b, 0, H + h)),
                  pl.BlockSpec((None, L, dh), lambda h, b, n: (b, 0, 2 * H + h)),
                  pl.BlockSpec((None, 1, rev_len), lambda h, b, n: (h, 0, 0))],
        out_specs=pl.BlockSpec((None, BS, dh), lambda h, b, n: (b, n, h)),
        scratch_shapes=[pltpu.VMEM((LANES, dh), F32),
                        pltpu.VMEM((L, LANES), BF16),
                        pltpu.VMEM((nb, BS, BS), F32),
                        pltpu.VMEM((nb, BS, BS), F32),
                        pltpu.VMEM((BS, BS // 2), F32),
                        pltpu.VMEM((BS, BS // 2), F32),
                        pltpu.VMEM((BS, dh), F32)],
        compiler_params=_cparams(("parallel", "arbitrary", "arbitrary")),
        name="moba",
    )(proj, proj, proj, bias_rev)
    return _matmul_residual(o.reshape(T, D), w_out, x)


def kernel(x, norm_g, final_norm_g, ffn_w_gate, ffn_w_up, ffn_w_down, s5_lam_re, s5_lam_im, s5_log_dt, s5_b_re, s5_b_im, s5_c_re, s5_c_im, s5_d, s5_w_glu, hg_w_in, hg_lower_bounds, hg_norm_g, hg_w_out, gdn_w_in, gdn_conv_w, gdn_a_log, gdn_dt_bias, gdn_norm_g, gdn_w_out, moba_w_in, moba_w_out, rel_bias_table):
    bsz, L, D = x.shape
    depth = norm_g.shape[0]
    h = x.reshape(bsz * L, D)
    wg, wu, wd = ffn_w_gate.astype(BF16), ffn_w_up.astype(BF16), ffn_w_down.astype(BF16)
    for i in range(depth):
        m, j = i % 4, i // 4
        h = _ffn(h, norm_g[i, 0], wg, wu, wd, i, 0)
        if m == 0:
            h = _s5_mixer(h, norm_g[i, 1], s5_lam_re[j], s5_lam_im[j], s5_log_dt[j], s5_b_re[j],
                          s5_b_im[j], s5_c_re[j], s5_c_im[j], s5_d[j], s5_w_glu[j], bsz)
        elif m == 1:
            h = _hgrn_mixer(h, norm_g[i, 1], hg_w_in[j], hg_lower_bounds, hg_norm_g[j], hg_w_out[j],
                            bsz, i)
        elif m == 2:
            h = _gdn_mixer(h, norm_g[i, 1], gdn_w_in[j], gdn_conv_w[j], gdn_a_log[j], gdn_dt_bias[j],
                           gdn_norm_g[j], gdn_w_out[j], bsz)
        else:
            h = _moba_mixer(h, norm_g[i, 1], moba_w_in[j], rel_bias_table, moba_w_out[j], bsz)
        h = _ffn(h, norm_g[i, 2], wg, wu, wd, i, 1, final_norm_g if i == depth - 1 else None)
    return h.reshape(bsz, L, D)
```
